```python
import math
import jax, jax.numpy as jnp
from jax import lax
import numpy as np

D_MODEL = 1024
BATCH = 8
SEQ = 4096
DEPTH = 2
DEC_BATCH = 32
DEC_SEQ = 1
PAST_LEN = 16384
PAGE_SIZE = 128

HEAD_DIM = 64
H_GDN = 6
H_FOX = 6
W_GDN = H_GDN * HEAD_DIM
W_FOX = H_FOX * HEAD_DIM
W_POOL = D_MODEL - W_GDN - W_FOX
POOL_WINDOWS = (2, 4, 8, 16)
N_POOL_GROUPS = len(POOL_WINDOWS)
POOL_GROUP = W_POOL // N_POOL_GROUPS
POOL_BUF = max(POOL_WINDOWS) - 1
CONV_W = 4
GDN_CHUNK = 64
Q_BLOCK = 128
D_FF = -(-8 * D_MODEL // (3 * 256)) * 256
SPLIT_SIZES = (3 * W_GDN, W_GDN, H_GDN, H_GDN, 3 * W_FOX, H_FOX, W_POOL)
IN_WIDTH = 3 * W_GDN + W_GDN + H_GDN + H_GDN + 3 * W_FOX + H_FOX + W_POOL
RMS_EPS = 1e-6
L2_EPS = 1e-6

kernel_name = "hymba_gdn_fox_pool_decode_step"


def rms_norm(x, g):
    x32 = x.astype(jnp.float32)
    y = x32 * lax.rsqrt(jnp.mean(x32 * x32, axis=-1, keepdims=True) + RMS_EPS)
    return (y * g.astype(jnp.float32)).astype(x.dtype)


def l2_normalize(x):
    x32 = x.astype(jnp.float32)
    return x32 * lax.rsqrt(jnp.sum(x32 * x32, axis=-1, keepdims=True) + L2_EPS)


def split_columns(a, sizes):
    out, start = [], 0
    for s in sizes:
        out.append(a[..., start:start + s])
        start += s
    return out


def gather_pages(pool, table):
    pages = pool[table]
    return pages.reshape((table.shape[0], table.shape[1] * pool.shape[1]) + pool.shape[2:])


def gdn_chunked(q, k, v, g, beta, s0):
    B, L, H, DK = q.shape
    DV = v.shape[-1]
    C = GDN_CHUNK
    n = -(-L // C)
    pad = n * C - L

    def blocks(a):
        a = jnp.pad(a, ((0, 0), (0, pad)) + ((0, 0),) * (a.ndim - 2))
        a = a.reshape((B, n, C) + a.shape[2:])
        return jnp.moveaxis(a, 3, 1)

    q, k, v, g, beta = (blocks(a) for a in (q, k, v, g, beta))
    G = jnp.cumsum(g, axis=-1)
    tri = jnp.tril(jnp.ones((C, C), dtype=bool))
    strict = jnp.tril(jnp.ones((C, C), dtype=bool), k=-1)
    diff = G[..., :, None] - G[..., None, :]
    decay = jnp.where(tri, jnp.exp(jnp.where(tri, diff, 0.0)), 0.0)
    k_beta = k * beta[..., None]
    a_mat = jnp.where(strict, jnp.einsum('bhnik,bhnjk->bhnij', k_beta, k) * decay, 0.0)
    rhs = jnp.concatenate([v * beta[..., None], k_beta * jnp.exp(G)[..., None]], axis=-1)
    sol = lax.linalg.triangular_solve(a_mat + jnp.eye(C, dtype=a_mat.dtype), rhs,
                                      left_side=True, lower=True, unit_diagonal=True)
    u, w = sol[..., :DV], sol[..., DV:]
    qk = jnp.einsum('bhnik,bhnjk->bhnij', q, k) * decay
    q_dec = q * jnp.exp(G)[..., None]
    g_last = G[..., -1]
    k_dec = k * jnp.exp(g_last[..., None] - G)[..., None]

    def step(S, xs_n):
        u_n, w_n, qk_n, qd_n, kd_n, gl_n = xs_n
        v_new = u_n - jnp.einsum('bhck,bhkv->bhcv', w_n, S)
        o_n = jnp.einsum('bhck,bhkv->bhcv', qd_n, S) + jnp.einsum('bhij,bhjv->bhiv', qk_n, v_new)
        S = S * jnp.exp(gl_n)[..., None, None] + jnp.einsum('bhck,bhcv->bhkv', kd_n, v_new)
        return S, o_n

    xs = tuple(jnp.moveaxis(a, 2, 0) for a in (u, w, qk, q_dec, k_dec, g_last))
    s_fin, o = lax.scan(step, s0, xs)
    o = jnp.transpose(o, (1, 0, 3, 2, 4)).reshape(B, n * C, H, DV)[:, :L]
    return o, s_fin


def fox_prompt(q, k, v, logf):
    B, S, H, D = q.shape
    scale = D ** -0.5
    c = jnp.cumsum(logf, axis=1).transpose(0, 2, 1)
    nb = S // Q_BLOCK
    key_pos = jnp.arange(S)

    def one_block(i):
        qi = lax.dynamic_slice_in_dim(q, i * Q_BLOCK, Q_BLOCK, axis=1)
        ci = lax.dynamic_slice_in_dim(c, i * Q_BLOCK, Q_BLOCK, axis=2)
        s = (jnp.einsum('bqhd,bkhd->bhqk', qi, k).astype(jnp.float32) * scale
             + ci[..., :, None] - c[..., None, :])
        q_pos = i * Q_BLOCK + jnp.arange(Q_BLOCK)
        s = jnp.where(key_pos[None, :] <= q_pos[:, None], s, -jnp.inf)
        p = jax.nn.softmax(s, axis=-1).astype(v.dtype)
        return jnp.einsum('bhqk,bkhd->bqhd', p, v)

    o = lax.map(one_block, jnp.arange(nb))
    return o.transpose(1, 0, 2, 3, 4).reshape(B, S, H * D)


def fox_sample(q, k, v, logf, k_past, v_past, logf_past):
    B, L, H, D = q.shape
    P = k_past.shape[1]
    scale = D ** -0.5
    c = jnp.cumsum(jnp.concatenate([logf_past.astype(jnp.float32), logf], axis=1), axis=1)
    c = c.transpose(0, 2, 1)
    bias = c[..., P:, None] - c[..., None, :]
    s_past = jnp.einsum('blhd,bphd->bhlp', q, k_past)
    s_new = jnp.einsum('blhd,bmhd->bhlm', q, k)
    s = jnp.concatenate([s_past, s_new], axis=-1).astype(jnp.float32) * scale + bias
    mask = jnp.concatenate([jnp.ones((L, P), dtype=bool), jnp.tril(jnp.ones((L, L), dtype=bool))], axis=1)
    s = jnp.where(mask, s, -jnp.inf)
    p = jax.nn.softmax(s, axis=-1).astype(v.dtype)
    o = (jnp.einsum('bhlp,bphd->blhd', p[..., :P], v_past)
         + jnp.einsum('bhlm,bmhd->blhd', p[..., P:], v))
    return o.reshape(B, L, H * D)


def multiscale_pool(u_ext, pos0, pool_w, pool_scale):
    B, T, _ = u_ext.shape
    L = T - POOL_BUF
    u32 = u_ext.astype(jnp.float32)
    cs = jnp.cumsum(jnp.pad(u32, ((0, 0), (1, 0), (0, 0))), axis=1)
    pos = pos0 + jnp.arange(L)
    outs = []
    for gi, w in enumerate(POOL_WINDOWS):
        sl = slice(gi * POOL_GROUP, (gi + 1) * POOL_GROUP)
        wsum = cs[:, POOL_BUF + 1:, sl] - cs[:, POOL_BUF + 1 - w:POOL_BUF + 1 - w + L, sl]
        cnt = jnp.minimum(pos + 1, w).astype(jnp.float32)
        outs.append(wsum / cnt[None, :, None] - u32[:, POOL_BUF:, sl])
    d = jnp.stack(outs, axis=2)
    y = jnp.einsum('blgc,gcd->blgd', d, pool_w.astype(jnp.float32)).reshape(B, L, W_POOL)
    return (y * pool_scale.astype(jnp.float32)).astype(u_ext.dtype)


def swiglu(h, w_gate_up, w_down):
    gu = h @ w_gate_up
    return (jax.nn.silu(gu[..., :D_FF]) * gu[..., D_FF:]) @ w_down


def decoder_layer(x, lp, conv_buf, s0, pool_buf, past, pos0):
    (g_mix, w_in, conv_w, a_log, dt_bias, gdn_g, fox_bf, pool_w, pool_scale,
     w_out, g_ffn, w_gate_up, w_down) = lp
    B, L, _ = x.shape
    h = rms_norm(x, g_mix)
    proj = h @ w_in
    a_qkv, a_z, a_beta, a_alpha, f_qkv, f_gate, p_u = split_columns(proj, SPLIT_SIZES)

    ext = jnp.concatenate([conv_buf.astype(a_qkv.dtype), a_qkv], axis=1)
    conv = jax.nn.silu(sum(ext[:, j:j + L] * conv_w[j] for j in range(CONV_W)))
    q_a, k_a, v_a = (conv[..., i * W_GDN:(i + 1) * W_GDN].reshape(B, L, H_GDN, HEAD_DIM) for i in range(3))
    q_a = l2_normalize(q_a) * (HEAD_DIM ** -0.5)
    k_a = l2_normalize(k_a)
    beta = jax.nn.sigmoid(a_beta.astype(jnp.float32))
    g = -jnp.exp(a_log.astype(jnp.float32)) * jax.nn.softplus(a_alpha.astype(jnp.float32) + dt_bias)
    o_a, s_new = gdn_chunked(q_a, k_a, v_a.astype(jnp.float32), g, beta, s0.astype(jnp.float32))
    z = jax.nn.silu(a_z.astype(jnp.float32)).reshape(B, L, H_GDN, HEAD_DIM)
    o_a = (rms_norm(o_a, gdn_g) * z).reshape(B, L, W_GDN).astype(x.dtype)

    q_f, k_f, v_f = (f_qkv[..., i * W_FOX:(i + 1) * W_FOX].reshape(B, L, H_FOX, HEAD_DIM) for i in range(3))
    logf = jax.nn.log_sigmoid(f_gate.astype(jnp.float32) + fox_bf.astype(jnp.float32))
    if past is None:
        o_f = fox_prompt(q_f, k_f, v_f, logf)
    else:
        o_f = fox_sample(q_f, k_f, v_f, logf, past[0], past[1], past[2])

    ext_p = jnp.concatenate([pool_buf.astype(p_u.dtype), p_u], axis=1)
    o_c = multiscale_pool(ext_p, pos0, pool_w, pool_scale)

    x = x + jnp.concatenate([o_a, o_f.astype(x.dtype), o_c], axis=-1) @ w_out
    x = x + swiglu(rms_norm(x, g_ffn), w_gate_up, w_down)
    new_state = (k_f, v_f, logf, s_new, ext[:, L:], ext_p[:, L:])
    return x, new_state


def setup_inputs(seed: int = 0) -> dict:
    key = jax.random.key(seed)
    ks = jax.random.split(key, 24)
    f32 = jnp.float32
    n_pages = PAST_LEN // PAGE_SIZE
    n_used = DEC_BATCH * n_pages
    n_pool = n_used + max(1, n_used // 4)

    def nrm(k, shape, s):
        return jax.random.normal(k, shape, f32) * s

    x_prompt = nrm(ks[0], (BATCH, SEQ, D_MODEL), 1.0)
    x_sample = nrm(ks[1], (DEC_BATCH, DEC_SEQ, D_MODEL), 1.0)
    cache_k = nrm(ks[2], (DEPTH, n_pool, PAGE_SIZE, H_FOX, HEAD_DIM), 1.0)
    cache_v = nrm(ks[3], (DEPTH, n_pool, PAGE_SIZE, H_FOX, HEAD_DIM), 1.0)
    cache_logf = jax.nn.log_sigmoid(2.0 + nrm(ks[4], (DEPTH, n_pool, PAGE_SIZE, H_FOX), 1.0))
    state_gdn = nrm(ks[5], (DEPTH, DEC_BATCH, H_GDN, HEAD_DIM, HEAD_DIM), 0.1)
    state_conv = nrm(ks[6], (DEPTH, DEC_BATCH, CONV_W - 1, 3 * W_GDN), 1.0)
    state_pool = nrm(ks[7], (DEPTH, DEC_BATCH, POOL_BUF, W_POOL), 1.0)
    page_table = jax.random.permutation(ks[8], n_pool)[:n_used].reshape(DEC_BATCH, n_pages).astype(jnp.int32)

    norm_mix_g = 1.0 + nrm(ks[9], (DEPTH, D_MODEL), 0.1)
    w_in = nrm(ks[10], (DEPTH, D_MODEL, IN_WIDTH), D_MODEL ** -0.5)
    conv_w = nrm(ks[11], (DEPTH, CONV_W, 3 * W_GDN), CONV_W ** -0.5)
    a_log = jnp.log(jax.random.uniform(ks[12], (DEPTH, H_GDN), f32, 1.0, 16.0))
    dt = jnp.exp(jax.random.uniform(ks[13], (DEPTH, H_GDN), f32, math.log(1e-3), math.log(1e-1)))
    dt_bias = dt + jnp.log(-jnp.expm1(-dt))
    gdn_norm_g = 1.0 + nrm(ks[14], (DEPTH, HEAD_DIM), 0.1)
    fox_bf = 2.0 + nrm(ks[15], (DEPTH, H_FOX), 0.5)
    pool_w = nrm(ks[16], (DEPTH, N_POOL_GROUPS, POOL_GROUP, POOL_GROUP), POOL_GROUP ** -0.5)
    pool_scale = 1.0 + nrm(ks[17], (DEPTH, W_POOL), 0.1)
    w_out = nrm(ks[18], (DEPTH, D_MODEL, D_MODEL), D_MODEL ** -0.5)
    norm_ffn_g = 1.0 + nrm(ks[19], (DEPTH, D_MODEL), 0.1)
    w_gate_up = nrm(ks[20], (DEPTH, D_MODEL, 2 * D_FF), D_MODEL ** -0.5)
    w_down = nrm(ks[21], (DEPTH, D_FF, D_MODEL), D_FF ** -0.5)
    final_norm_g = 1.0 + nrm(ks[22], (D_MODEL,), 0.1)
    return {"x_prompt": x_prompt, "x_sample": x_sample,
            "cache_k": cache_k, "cache_v": cache_v, "cache_logf": cache_logf,
            "state_gdn": state_gdn, "state_conv": state_conv, "state_pool": state_pool,
            "page_table": page_table,
            "norm_mix_g": norm_mix_g, "w_in": w_in, "conv_w": conv_w, "a_log": a_log,
            "dt_bias": dt_bias, "gdn_norm_g": gdn_norm_g, "fox_bf": fox_bf,
            "pool_w": pool_w, "pool_scale": pool_scale, "w_out": w_out,
            "norm_ffn_g": norm_ffn_g, "w_gate_up": w_gate_up, "w_down": w_down,
            "final_norm_g": final_norm_g}


def reference(x_prompt, x_sample, cache_k, cache_v, cache_logf, state_gdn, state_conv, state_pool,
              page_table, norm_mix_g, w_in, conv_w, a_log, dt_bias, gdn_norm_g, fox_bf,
              pool_w, pool_scale, w_out, norm_ffn_g, w_gate_up, w_down, final_norm_g):
    xp, xs = x_prompt, x_sample
    bp = x_prompt.shape[0]
    st_p, st_s = [], []
    for l in range(DEPTH):
        lp = (norm_mix_g[l], w_in[l], conv_w[l], a_log[l], dt_bias[l], gdn_norm_g[l], fox_bf[l],
              pool_w[l], pool_scale[l], w_out[l], norm_ffn_g[l], w_gate_up[l], w_down[l])
        conv0 = jnp.zeros((bp, CONV_W - 1, 3 * W_GDN), xp.dtype)
        s0 = jnp.zeros((bp, H_GDN, HEAD_DIM, HEAD_DIM), jnp.float32)
        pool0 = jnp.zeros((bp, POOL_BUF, W_POOL), xp.dtype)
        xp, sp = decoder_layer(xp, lp, conv0, s0, pool0, None, 0)
        past = (gather_pages(cache_k[l], page_table), gather_pages(cache_v[l], page_table),
                gather_pages(cache_logf[l], page_table))
        xs, ss = decoder_layer(xs, lp, state_conv[l], state_gdn[l], state_pool[l], past, PAST_LEN)
        st_p.append(sp)
        st_s.append(ss)
    y_prompt = rms_norm(xp, final_norm_g)
    y_sample = rms_norm(xs, final_norm_g)
    p_k = jnp.stack([s[0] for s in st_p])
    p_v = jnp.stack([s[1] for s in st_p])
    p_logf = jnp.stack([s[2] for s in st_p])
    p_gdn = jnp.stack([s[3] for s in st_p])
    p_conv = jnp.stack([s[4] for s in st_p])
    p_pool = jnp.stack([s[5] for s in st_p])
    s_k = jnp.stack([s[0] for s in st_s])
    s_v = jnp.stack([s[1] for s in st_s])
    s_logf = jnp.stack([s[2] for s in st_s])
    s_gdn = jnp.stack([s[3] for s in st_s])
    s_conv = jnp.stack([s[4] for s in st_s])
    s_pool = jnp.stack([s[5] for s in st_s])
    return (y_prompt, y_sample, p_k, p_v, p_logf, p_gdn, p_conv, p_pool,
            s_k, s_v, s_logf, s_gdn, s_conv, s_pool)
```

```python
import functools

import jax
import jax.numpy as jnp
from jax import lax
from jax.experimental import pallas as pl
from jax.experimental.pallas import tpu as pltpu

HEAD_DIM = 64
POOL_WINDOWS = (2, 4, 8, 16)
POOL_HIST = 16
GDN_CHUNK = 64
RMS_EPS = 1e-6
L2_EPS = 1e-6
LANES = 128
VMEM_LIMIT = 56 * 1024 * 1024
HI = lax.Precision.HIGHEST

GATE_BETA = 0
GATE_G = 6
GATE_LOGF = 12


def _cparams(sem):
    return pltpu.CompilerParams(dimension_semantics=sem, vmem_limit_bytes=VMEM_LIMIT)


def _const_spec(shape):
    nd = len(shape)
    return pl.BlockSpec(shape, lambda *_: (0,) * nd, pipeline_mode=pl.Buffered(1))


def _softplus(x):
    return jnp.maximum(x, 0.0) + jnp.log1p(jnp.exp(-jnp.abs(x)))


def _silu(x):
    return x * jax.nn.sigmoid(x)


def _dot_nt(a, b, precision=None):
    return lax.dot_general(a, b, (((1,), (1,)), ((), ())), precision=precision,
                           preferred_element_type=jnp.float32)


def _dot(a, b, precision=None):
    return jnp.dot(a, b, precision=precision, preferred_element_type=jnp.float32)


def _in_proj_kernel(x_ref, g_ref, w_ref, gp_ref, qkv_ref, z_ref, fq_ref, fk_ref, fv_ref, pu_ref,
                    gt_ref, cum_ref, carry_ref, *, seg, tiles_per_seq, tm):
    i = pl.program_id(0)
    x = x_ref[...]
    h = x * lax.rsqrt(jnp.mean(x * x, axis=-1, keepdims=True) + RMS_EPS) * g_ref[...]
    hb = h.astype(jnp.bfloat16)
    outs = (qkv_ref, z_ref, fq_ref, fk_ref, fv_ref, pu_ref)
    for o_ref, (c0, c1) in zip(outs, seg[:-1]):
        o_ref[...] = _dot(hb, w_ref[:, c0:c1])
    c0, c1 = seg[-1]
    raw = _dot(hb, w_ref[:, c0:c1])
    lane = lax.broadcasted_iota(jnp.int32, raw.shape, 1)
    a_log = gp_ref[0:1, :]
    shifted = raw + gp_ref[1:2, :]
    beta = jax.nn.sigmoid(raw)
    g = -jnp.exp(a_log) * _softplus(shifted)
    logf = -_softplus(-shifted)
    gt = jnp.where(lane < GATE_G, beta,
                   jnp.where(lane < GATE_LOGF, g,
                             jnp.where(lane < GATE_LOGF + 6, logf, 0.0)))
    gt_ref[...] = gt

    @pl.when(i % tiles_per_seq == 0)
    def _():
        carry_ref[...] = jnp.zeros_like(carry_ref)

    sub = min(tm, LANES)
    r = lax.broadcasted_iota(jnp.int32, (sub, sub), 0)
    c = lax.broadcasted_iota(jnp.int32, (sub, sub), 1)
    tril = (c <= r).astype(jnp.float32)
    carry = carry_ref[...]
    for s in range(tm // sub):
        cs = _dot(tril, gt[s * sub:(s + 1) * sub], HI) + carry
        cum_ref[s * sub:(s + 1) * sub, :] = cs
        carry = cs[sub - 1:sub, :]
    carry_ref[...] = carry


def _in_proj(x2d, g_mix, w_perm, gate_params, seg, seq_len, tm):
    n, d = x2d.shape
    widths = [c1 - c0 for c0, c1 in seg]
    out_shape = [jax.ShapeDtypeStruct((n, w), jnp.float32) for w in widths]
    out_shape.append(jax.ShapeDtypeStruct((n, LANES), jnp.float32))
    row_spec = lambda w: pl.BlockSpec((tm, w), lambda i: (i, 0))
    return pl.pallas_call(
        functools.partial(_in_proj_kernel, seg=seg, tiles_per_seq=max(seq_len // tm, 1), tm=tm),
        grid=(n // tm,),
        in_specs=[row_spec(d), _const_spec((1, d)), _const_spec(w_perm.shape), _const_spec((8, LANES))],
        out_specs=[row_spec(w) for w in widths] + [row_spec(LANES)],
        out_shape=out_shape,
        scratch_shapes=[pltpu.VMEM((1, LANES), jnp.float32)],
        compiler_params=_cparams(("arbitrary",)),
    )(x2d, g_mix, w_perm, gate_params)


def _forward_substitute(a_blocks, x_blocks, n):
    for j in range(n - 1):
        row = x_blocks[j // 8][j % 8:j % 8 + 1, :]
        for rr in range((j + 1) // 8, n // 8):
            x_blocks[rr] = x_blocks[rr] - a_blocks[rr][:, j:j + 1] * row
    return x_blocks


def _gdn_kernel(qkv_ref, z_ref, gt_ref, cw_ref, conv0_ref, s0_ref, gn_ref, o_ref, s_ref, hist_ref,
                *, n_heads, valid):
    c = pl.program_id(1)
    C = GDN_CHUNK
    D = HEAD_DIM
    W = n_heads * D

    @pl.when(c == 0)
    def _():
        hist_ref[...] = conv0_ref[0]
        s_ref[...] = s0_ref[...]

    x = qkv_ref[0]
    ext = jnp.concatenate([hist_ref[...], x], axis=0)
    conv = ext[5:5 + C] * cw_ref[0:1, :]
    for j in range(1, 4):
        conv = conv + ext[5 + j:5 + j + C] * cw_ref[j:j + 1, :]
    conv = _silu(conv)
    hist_ref[...] = x[C - 8:C]

    gt = gt_ref[0]
    if valid < C:
        live = lax.broadcasted_iota(jnp.int32, (C, 1), 0) < valid
        conv = jnp.where(live, conv, 0.0)
        gt = jnp.where(live, gt, 0.0)

    ri = lax.broadcasted_iota(jnp.int32, (C, C), 0)
    ci = lax.broadcasted_iota(jnp.int32, (C, C), 1)
    tri = ci <= ri
    strict = ci < ri
    g_cum = _dot(tri.astype(jnp.float32), gt, HI)
    g_cum_t = g_cum.T

    outs = []
    for h in range(n_heads):
        q = conv[:, h * D:(h + 1) * D]
        k = conv[:, W + h * D:W + (h + 1) * D]
        v = conv[:, 2 * W + h * D:2 * W + (h + 1) * D]
        q = q * lax.rsqrt(jnp.sum(q * q, axis=-1, keepdims=True) + L2_EPS) * (D ** -0.5)
        k = k * lax.rsqrt(jnp.sum(k * k, axis=-1, keepdims=True) + L2_EPS)
        beta = gt[:, GATE_BETA + h:GATE_BETA + h + 1]
        gc = g_cum[:, GATE_G + h:GATE_G + h + 1]
        gr = g_cum_t[GATE_G + h:GATE_G + h + 1, :]
        g_last = gc[C - 1:C, :]
        decay = jnp.where(tri, jnp.exp(jnp.where(tri, gc - gr, 0.0)), 0.0)
        e_g = jnp.exp(gc)
        kb = k * beta
        a_mat = jnp.where(strict, _dot_nt(kb, k) * decay, 0.0)
        rhs = jnp.concatenate([v * beta, kb * e_g], axis=1)
        a_blocks = [a_mat[r * 8:(r + 1) * 8, :] for r in range(C // 8)]
        x_blocks = [rhs[r * 8:(r + 1) * 8, :] for r in range(C // 8)]
        sol = jnp.concatenate(_forward_substitute(a_blocks, x_blocks, C), axis=0)
        u = sol[:, :D]
        w = sol[:, D:]
        qk = _dot_nt(q, k) * decay
        qd = q * e_g
        kd = k * jnp.exp(g_last - gc)
        s_prev = s_ref[0, h]
        v_new = u - _dot(w, s_prev)
        o = _dot(qd, s_prev) + _dot(qk, v_new)
        s_ref[0, h] = s_prev * jnp.exp(g_last) + _dot(kd.T, v_new)
        o = o * lax.rsqrt(jnp.mean(o * o, axis=-1, keepdims=True) + RMS_EPS) * gn_ref[...]
        outs.append(o * _silu(z_ref[0][:, h * D:(h + 1) * D]))
    o_ref[0] = jnp.concatenate(outs, axis=1)


def _gdn(qkv, z, gt, conv_w, conv0, s0, gdn_g, valid):
    b, l, w3 = qkv.shape
    w = w3 // 3
    n_heads = w // HEAD_DIM
    C = GDN_CHUNK
    tok = lambda width: pl.BlockSpec((1, C, width), lambda i, j: (i, j, 0))
    per_b = lambda shape: pl.BlockSpec((1,) + shape, lambda i, j: (i,) + (0,) * len(shape))
    return pl.pallas_call(
        functools.partial(_gdn_kernel, n_heads=n_heads, valid=valid),
        grid=(b, l // C),
        in_specs=[tok(w3), tok(w), tok(LANES), _const_spec((4, w3)), per_b((8, w3)),
                  per_b((n_heads, HEAD_DIM, HEAD_DIM)), _const_spec((1, HEAD_DIM))],
        out_specs=[tok(w), per_b((n_heads, HEAD_DIM, HEAD_DIM))],
        out_shape=[jax.ShapeDtypeStruct((b, l, w), jnp.float32),
                   jax.ShapeDtypeStruct((b, n_heads, HEAD_DIM, HEAD_DIM), jnp.float32)],
        scratch_shapes=[pltpu.VMEM((8, w3), jnp.float32)],
        compiler_params=_cparams(("arbitrary", "arbitrary")),
    )(qkv, z, gt, conv_w, conv0, s0, gdn_g)


def _fox_prompt_kernel(q_ref, k_ref, v_ref, cum_ref, cum_t_ref, o_ref, m_ref, l_ref, acc_ref, *, tq):
    hp = pl.program_id(1)
    qi = pl.program_id(2)
    lane = lax.broadcasted_iota(jnp.int32, (tq, LANES), 1)
    low = lane < HEAD_DIM
    q = q_ref[0] * (HEAD_DIM ** -0.5)
    q_heads = (jnp.where(low, q, 0.0).astype(jnp.bfloat16),
               jnp.where(low, 0.0, q).astype(jnp.bfloat16))
    cum = cum_ref[0]
    head_lane = lax.broadcasted_iota(jnp.int32, cum.shape, 1) - GATE_LOGF - 2 * hp
    c_rows = [jnp.sum(jnp.where(head_lane == hh, cum, 0.0), axis=1, keepdims=True) for hh in range(2)]

    m_ref[...] = jnp.full(m_ref.shape, -jnp.inf, jnp.float32)
    l_ref[...] = jnp.zeros(l_ref.shape, jnp.float32)
    acc_ref[...] = jnp.zeros(acc_ref.shape, jnp.float32)

    def tile(ki, masked):
        ks = pl.multiple_of(ki * tq, tq)
        kb = k_ref[0, pl.ds(ks, tq), :].astype(jnp.bfloat16)
        vb = v_ref[0, pl.ds(ks, tq), :].astype(jnp.bfloat16)
        c_cols = cum_t_ref[0, :, pl.ds(ks, tq)]
        for hh in range(2):
            sub = lax.broadcasted_iota(jnp.int32, c_cols.shape, 0)
            c_col = jnp.sum(jnp.where(sub == 2 * hp + hh, c_cols, 0.0), axis=0, keepdims=True)
            s = _dot_nt(q_heads[hh], kb) + c_rows[hh] - c_col
            if masked:
                r = lax.broadcasted_iota(jnp.int32, s.shape, 0)
                cc = lax.broadcasted_iota(jnp.int32, s.shape, 1)
                s = jnp.where(cc <= r, s, -jnp.inf)
            m_prev = m_ref[hh]
            m_new = jnp.maximum(m_prev, jnp.max(s, axis=1, keepdims=True))
            p = jnp.exp(s - m_new)
            alpha = jnp.exp(m_prev - m_new)
            l_ref[hh] = alpha * l_ref[hh] + jnp.sum(p, axis=1, keepdims=True)
            acc_ref[hh] = alpha * acc_ref[hh] + _dot(p.astype(jnp.bfloat16), vb)
            m_ref[hh] = m_new

    def body(ki, carry):
        tile(ki, False)
        return carry

    lax.fori_loop(0, qi, body, 0)
    tile(qi, True)
    o_ref[0] = jnp.where(low, acc_ref[0] / l_ref[0], acc_ref[1] / l_ref[1])


def _fox_prompt(fq, fk, fv, cum, cum_t, tq):
    b, l, w = fq.shape
    n_pairs = w // LANES
    q_spec = pl.BlockSpec((1, tq, LANES), lambda i, hp, qi: (i, qi, hp))
    kv_spec = pl.BlockSpec((1, l, LANES), lambda i, hp, qi: (i, 0, hp))
    return pl.pallas_call(
        functools.partial(_fox_prompt_kernel, tq=tq),
        grid=(b, n_pairs, l // tq),
        in_specs=[q_spec, kv_spec, kv_spec,
                  pl.BlockSpec((1, tq, LANES), lambda i, hp, qi: (i, qi, 0)),
                  pl.BlockSpec((1, 8, l), lambda i, hp, qi: (i, 0, 0))],
        out_specs=q_spec,
        out_shape=jax.ShapeDtypeStruct((b, l, w), jnp.float32),
        scratch_shapes=[pltpu.VMEM((2, tq, 1), jnp.float32), pltpu.VMEM((2, tq, 1), jnp.float32),
                        pltpu.VMEM((2, tq, LANES), jnp.float32)],
        compiler_params=_cparams(("arbitrary", "arbitrary", "arbitrary")),
    )(fq, fk, fv, cum, cum_t)


def _fox_sample_kernel(pt_ref, q_ref, kn_ref, vn_ref, gt_ref, *refs, pages_per_step, n_heads):
    del pt_ref
    pp = pages_per_step
    k_refs, v_refs, lf_refs = refs[:pp], refs[pp:2 * pp], refs[2 * pp:3 * pp]
    o_ref, m_ref, l_ref, run_ref, acc_ref = refs[3 * pp:]
    j = pl.program_id(1)
    w = n_heads * HEAD_DIM
    page = k_refs[0].shape[1]

    sub = lax.broadcasted_iota(jnp.int32, (8, w), 0)
    own = lax.broadcasted_iota(jnp.int32, (8, w), 1) // HEAD_DIM == sub
    qb = jnp.where(own, q_ref[0] * (HEAD_DIM ** -0.5), 0.0)

    @pl.when(j == 0)
    def _():
        m_ref[...] = jnp.sum(qb * kn_ref[0], axis=1, keepdims=True)
        l_ref[...] = jnp.ones_like(l_ref)
        acc_ref[...] = jnp.broadcast_to(vn_ref[0], acc_ref.shape)
        gt = gt_ref[0]
        pick = (lax.broadcasted_iota(jnp.int32, (8, LANES), 1) - GATE_LOGF
                == lax.broadcasted_iota(jnp.int32, (8, LANES), 0))
        run_ref[...] = jnp.sum(jnp.where(pick, gt, 0.0), axis=1, keepdims=True)

    r = lax.broadcasted_iota(jnp.int32, (page, page), 0)
    c = lax.broadcasted_iota(jnp.int32, (page, page), 1)
    later = (r > c).astype(jnp.float32)
    qbb = qb.astype(jnp.bfloat16)
    run = run_ref[...]
    scores = []
    for i in range(pp):
        lf = lf_refs[i][0]
        s = _dot_nt(qbb, k_refs[i][0].astype(jnp.bfloat16))
        scores.append(s + run + _dot(lf, later, HI))
        run = run + jnp.sum(lf, axis=1, keepdims=True)
    run_ref[...] = run
    s_all = jnp.concatenate(scores, axis=1)
    m_prev = m_ref[...]
    m_new = jnp.maximum(m_prev, jnp.max(s_all, axis=1, keepdims=True))
    p = jnp.exp(s_all - m_new)
    alpha = jnp.exp(m_prev - m_new)
    l_ref[...] = alpha * l_ref[...] + jnp.sum(p, axis=1, keepdims=True)
    pv = _dot(p[:, :page].astype(jnp.bfloat16), v_refs[0][0].astype(jnp.bfloat16))
    for i in range(1, pp):
        pv = pv + _dot(p[:, i * page:(i + 1) * page].astype(jnp.bfloat16),
                       v_refs[i][0].astype(jnp.bfloat16))
    acc_ref[...] = alpha * acc_ref[...] + pv
    m_ref[...] = m_new

    @pl.when(j == pl.num_programs(1) - 1)
    def _():
        o = jnp.where(own, acc_ref[...] / l_ref[...], 0.0)
        o_ref[0] = jnp.sum(o, axis=0, keepdims=True)


def _fox_sample(page_table, fq, fk, fv, gt, cache_k, cache_v, cache_lf_t, pages_per_step):
    b, _, w = fq.shape
    n_heads = w // HEAD_DIM
    _, page, _ = cache_k.shape
    n_pages = page_table.shape[1]
    pp = pages_per_step

    def page_map(i):
        return lambda bi, j, pt: (pt[bi, n_pages - 1 - (j * pp + i)], 0, 0)

    tok = lambda width: pl.BlockSpec((1, 1, width), lambda bi, j, pt: (bi, 0, 0))
    in_specs = [tok(w), tok(w), tok(w), tok(LANES)]
    in_specs += [pl.BlockSpec((1, page, w), page_map(i)) for i in range(pp)]
    in_specs += [pl.BlockSpec((1, page, w), page_map(i)) for i in range(pp)]
    in_specs += [pl.BlockSpec((1, 8, page), page_map(i)) for i in range(pp)]
    grid_spec = pltpu.PrefetchScalarGridSpec(
        num_scalar_prefetch=1, grid=(b, n_pages // pp), in_specs=in_specs, out_specs=tok(w),
        scratch_shapes=[pltpu.VMEM((8, 1), jnp.float32), pltpu.VMEM((8, 1), jnp.float32),
                        pltpu.VMEM((8, 1), jnp.float32), pltpu.VMEM((8, w), jnp.float32)])
    return pl.pallas_call(
        functools.partial(_fox_sample_kernel, pages_per_step=pp, n_heads=n_heads),
        grid_spec=grid_spec,
        out_shape=jax.ShapeDtypeStruct((b, 1, w), jnp.float32),
        compiler_params=_cparams(("arbitrary", "arbitrary")),
    )(page_table, fq, fk, fv, gt, *([cache_k] * pp), *([cache_v] * pp), *([cache_lf_t] * pp))


def _pool_kernel(u_ref, hist0_ref, w_ref, scale_ref, o_ref, hist_ref, *, pos0, tl):
    t = pl.program_id(1)

    @pl.when(t == 0)
    def _():
        hist_ref[...] = hist0_ref[0]

    u = u_ref[0]
    ext = jnp.concatenate([hist_ref[...], u], axis=0)
    hist_ref[...] = ext[tl:tl + POOL_HIST]
    sums = []
    acc = ext
    for step in (1, 2, 4, 8):
        acc = acc + pltpu.roll(acc, step, axis=0)
        sums.append(acc[POOL_HIST:])
    width = u.shape[1]
    group = lax.broadcasted_iota(jnp.int32, (tl, width), 1) // (width // len(POOL_WINDOWS))
    wsum = jnp.where(group == 0, sums[0], jnp.where(group == 1, sums[1],
                                                    jnp.where(group == 2, sums[2], sums[3])))
    window = jnp.where(group == 0, POOL_WINDOWS[0],
                       jnp.where(group == 1, POOL_WINDOWS[1],
                                 jnp.where(group == 2, POOL_WINDOWS[2], POOL_WINDOWS[3])))
    pos = pos0 + t * tl + lax.broadcasted_iota(jnp.int32, (tl, width), 0)
    cnt = jnp.minimum(pos + 1, window).astype(jnp.float32)
    d = wsum / cnt - u
    o_ref[0] = _dot(d.astype(jnp.bfloat16), w_ref[...]) * scale_ref[...]


def _pool(u, hist0, w_bd, scale, pos0, tl):
    b, l, w = u.shape
    return pl.pallas_call(
        functools.partial(_pool_kernel, pos0=pos0, tl=tl),
        grid=(b, l // tl),
        in_specs=[pl.BlockSpec((1, tl, w), lambda i, t: (i, t, 0)),
                  pl.BlockSpec((1, POOL_HIST, w), lambda i, t: (i, 0, 0)),
                  _const_spec((w, w)), _const_spec((1, w))],
        out_specs=pl.BlockSpec((1, tl, w), lambda i, t: (i, t, 0)),
        out_shape=jax.ShapeDtypeStruct((b, l, w), jnp.float32),
        scratch_shapes=[pltpu.VMEM((POOL_HIST, w), jnp.float32)],
        compiler_params=_cparams(("arbitrary", "arbitrary")),
    )(u, hist0, w_bd, scale)


def _tail_kernel(x_ref, oa_ref, of_ref, oc_ref, wo_ref, gf_ref, wg_ref, wu_ref, wd_ref, gl_ref, o_ref,
                 *, ff_chunks, final_norm):
    bf = jnp.bfloat16
    mixed = jnp.concatenate([oa_ref[...], of_ref[...], oc_ref[...]], axis=1).astype(bf)
    x = x_ref[...] + _dot(mixed, wo_ref[...])
    h = (x * lax.rsqrt(jnp.mean(x * x, axis=-1, keepdims=True) + RMS_EPS) * gf_ref[...]).astype(bf)
    for c0, c1 in ff_chunks:
        act = _silu(_dot(h, wg_ref[:, c0:c1])) * _dot(h, wu_ref[:, c0:c1])
        x = x + _dot(act.astype(bf), wd_ref[c0:c1, :])
    if final_norm:
        x = x * lax.rsqrt(jnp.mean(x * x, axis=-1, keepdims=True) + RMS_EPS) * gl_ref[...]
    o_ref[...] = x


def _tail(x2d, oa, of, oc, wo, g_ffn, wg, wu, wd, g_last, final_norm, tm):
    n, d = x2d.shape
    d_ff = wg.shape[1]
    step = 1024
    ff_chunks = tuple((c0, min(c0 + step, d_ff)) for c0 in range(0, d_ff, step))
    row_spec = lambda w: pl.BlockSpec((tm, w), lambda i: (i, 0))
    weights = (wo, g_ffn, wg, wu, wd, g_last)
    return pl.pallas_call(
        functools.partial(_tail_kernel, ff_chunks=ff_chunks, final_norm=final_norm),
        grid=(n // tm,),
        in_specs=[row_spec(d), row_spec(oa.shape[1]), row_spec(of.shape[1]), row_spec(oc.shape[1])]
        + [_const_spec(a.shape) for a in weights],
        out_specs=row_spec(d),
        out_shape=jax.ShapeDtypeStruct((n, d), jnp.float32),
        compiler_params=_cparams(("arbitrary",)),
    )(x2d, oa, of, oc, *weights)


def _layer_weights(l, w_gdn, w_fox, w_pool, norm_mix_g, w_in, conv_w, a_log, dt_bias, gdn_norm_g,
                   fox_bf, pool_w, pool_scale, w_out, norm_ffn_g, w_gate_up, w_down):
    bf = jnp.bfloat16
    n_hg = w_gdn // HEAD_DIM
    n_hf = w_fox // HEAD_DIM
    d_ff = w_down.shape[1]
    o = 0
    cols = {}
    for name, width in (("qkv", 3 * w_gdn), ("z", w_gdn), ("beta", n_hg), ("alpha", n_hg),
                        ("fqkv", 3 * w_fox), ("fgate", n_hf), ("pu", w_pool)):
        cols[name] = (o, o + width)
        o += width
    wl = w_in[l]
    sl = lambda name: wl[:, cols[name][0]:cols[name][1]]
    gate_cols = jnp.concatenate([sl("beta"), sl("alpha"), sl("fgate")], axis=1)
    gate_cols = jnp.pad(gate_cols, ((0, 0), (0, LANES - gate_cols.shape[1])))
    w_perm = jnp.concatenate([sl("qkv"), sl("z"), sl("fqkv"), sl("pu"), gate_cols], axis=1).astype(bf)
    seg, o = [], 0
    for width in (3 * w_gdn, w_gdn, w_fox, w_fox, w_fox, w_pool, LANES):
        seg.append((o, o + width))
        o += width
    gp = jnp.zeros((8, LANES), jnp.float32)
    gp = gp.at[0, GATE_G:GATE_G + n_hg].set(a_log[l])
    gp = gp.at[1, GATE_G:GATE_G + n_hg].set(dt_bias[l])
    gp = gp.at[1, GATE_LOGF:GATE_LOGF + n_hf].set(fox_bf[l])
    n_groups, pg, _ = pool_w[l].shape
    w_bd = jnp.zeros((w_pool, w_pool), jnp.float32)
    for gi in range(n_groups):
        w_bd = w_bd.at[gi * pg:(gi + 1) * pg, gi * pg:(gi + 1) * pg].set(pool_w[l, gi])
    return dict(
        g_mix=norm_mix_g[l][None, :], w_perm=w_perm, seg=tuple(seg), gp=gp, conv_w=conv_w[l],
        gdn_g=gdn_norm_g[l][None, :], w_bd=w_bd.astype(bf), pool_scale=pool_scale[l][None, :],
        wo=w_out[l].astype(bf),
        g_ffn=norm_ffn_g[l][None, :], wg=w_gate_up[l][:, :d_ff].astype(bf),
        wu=w_gate_up[l][:, d_ff:].astype(bf), wd=w_down[l].astype(bf))


def kernel(x_prompt, x_sample, cache_k, cache_v, cache_logf, state_gdn, state_conv, state_pool,
           page_table, norm_mix_g, w_in, conv_w, a_log, dt_bias, gdn_norm_g, fox_bf, pool_w,
           pool_scale, w_out, norm_ffn_g, w_gate_up, w_down, final_norm_g):
    f32 = jnp.float32
    bp, seq, d = x_prompt.shape
    bs, dec_seq, _ = x_sample.shape
    depth, n_pool, page, n_hf, _ = cache_k.shape
    n_hg = state_gdn.shape[2]
    w_gdn = n_hg * HEAD_DIM
    w_fox = n_hf * HEAD_DIM
    w_pool = state_pool.shape[-1]
    pool_buf = state_pool.shape[2]
    conv_hist = state_conv.shape[2]
    past_len = page_table.shape[1] * page
    C = GDN_CHUNK
    assert dec_seq == 1 and seq % C == 0 and conv_hist == 3 and pool_buf == POOL_HIST - 1

    tm_p = min(512, seq)
    tq = min(512, seq)
    tl = min(512, seq)
    pps = 8 if page_table.shape[1] % 8 == 0 else 1
    g_last = final_norm_g[None, :]

    xp = x_prompt.reshape(bp * seq, d)
    xs = x_sample.reshape(bs, d)
    cache_k2 = cache_k.reshape(depth, n_pool, page, w_fox)
    cache_v2 = cache_v.reshape(depth, n_pool, page, w_fox)
    cache_lf_t = jnp.pad(jnp.swapaxes(cache_logf, 2, 3), ((0, 0), (0, 0), (0, 8 - n_hf), (0, 0)))

    st_p, st_s = [], []
    for l in range(depth):
        lw = _layer_weights(l, w_gdn, w_fox, w_pool, norm_mix_g, w_in, conv_w, a_log, dt_bias,
                            gdn_norm_g, fox_bf, pool_w, pool_scale, w_out, norm_ffn_g, w_gate_up, w_down)
        final = l == depth - 1

        qkv, z, fq, fk, fv, pu, gt, cum = _in_proj(xp, lw["g_mix"], lw["w_perm"], lw["gp"], lw["seg"],
                                                   seq, tm_p)
        r3 = lambda a: a.reshape(bp, seq, a.shape[-1])
        o_a, s_fin = _gdn(r3(qkv), r3(z), r3(gt), lw["conv_w"], jnp.zeros((bp, 8, 3 * w_gdn), f32),
                          jnp.zeros((bp, n_hg, HEAD_DIM, HEAD_DIM), f32), lw["gdn_g"], C)
        cum3 = r3(cum)
        cum_t = jnp.swapaxes(cum3[:, :, GATE_LOGF:GATE_LOGF + 8], 1, 2)
        o_f = _fox_prompt(r3(fq), r3(fk), r3(fv), cum3, cum_t, tq)
        o_c = _pool(r3(pu), jnp.zeros((bp, POOL_HIST, w_pool), f32), lw["w_bd"], lw["pool_scale"], 0, tl)
        xp = _tail(xp, o_a.reshape(bp * seq, w_gdn), o_f.reshape(bp * seq, w_fox),
                   o_c.reshape(bp * seq, w_pool), lw["wo"], lw["g_ffn"],
                   lw["wg"], lw["wu"], lw["wd"], g_last, final, tm_p if tm_p <= 256 else 256)
        st_p.append((fk.reshape(bp, seq, n_hf, HEAD_DIM), fv.reshape(bp, seq, n_hf, HEAD_DIM),
                     r3(gt)[:, :, GATE_LOGF:GATE_LOGF + n_hf], s_fin,
                     r3(qkv)[:, seq - conv_hist:], r3(pu)[:, seq - pool_buf:]))

        qkv, z, fq, fk, fv, pu, gt, _ = _in_proj(xs, lw["g_mix"], lw["w_perm"], lw["gp"], lw["seg"], 1, bs)
        pad_c = lambda a: jnp.pad(a[:, None, :], ((0, 0), (0, C - 1), (0, 0)))
        conv0 = jnp.pad(state_conv[l], ((0, 0), (8 - conv_hist, 0), (0, 0)))
        o_a, s_fin = _gdn(pad_c(qkv), pad_c(z), pad_c(gt), lw["conv_w"], conv0, state_gdn[l],
                          lw["gdn_g"], 1)
        o_f = _fox_sample(page_table, fq[:, None, :], fk[:, None, :], fv[:, None, :], gt[:, None, :],
                          cache_k2[l], cache_v2[l], cache_lf_t[l], pps)
        hist0 = jnp.pad(state_pool[l], ((0, 0), (POOL_HIST - pool_buf, 0), (0, 0)))
        o_c = _pool(jnp.pad(pu[:, None, :], ((0, 0), (0, 7), (0, 0))), hist0, lw["w_bd"],
                    lw["pool_scale"], past_len, 8)
        xs = _tail(xs, o_a[:, 0], o_f[:, 0], o_c[:, 0], lw["wo"], lw["g_ffn"],
                   lw["wg"], lw["wu"], lw["wd"], g_last, final, bs)
        st_s.append((fk.reshape(bs, 1, n_hf, HEAD_DIM), fv.reshape(bs, 1, n_hf, HEAD_DIM),
                     gt[:, None, GATE_LOGF:GATE_LOGF + n_hf], s_fin,
                     jnp.concatenate([state_conv[l][:, 1:], qkv[:, None, :]], axis=1),
                     jnp.concatenate([state_pool[l][:, 1:], pu[:, None, :]], axis=1)))

    outs = [xp.reshape(bp, seq, d), xs.reshape(bs, 1, d)]
    for st in (st_p, st_s):
        for i in range(6):
            outs.append(jnp.stack([s[i] for s in st]))
    return tuple(outs)
```

```python
import functools

import jax
import jax.numpy as jnp
from jax import lax
from jax.experimental import pallas as pl
from jax.experimental.pallas import tpu as pltpu

HEAD_DIM = 64
POOL_WINDOWS = (2, 4, 8, 16)
POOL_HIST = 16
GDN_CHUNK = 64
RMS_EPS = 1e-6
L2_EPS = 1e-6
LANES = 128
VMEM_LIMIT = 56 * 1024 * 1024
HI = lax.Precision.HIGHEST

GATE_BETA = 0
GATE_G = 6
GATE_LOGF = 12


def _cparams(sem):
    return pltpu.CompilerParams(dimension_semantics=sem, vmem_limit_bytes=VMEM_LIMIT)


def _const_spec(shape):
    nd = len(shape)
    return pl.BlockSpec(shape, lambda *_: (0,) * nd, pipeline_mode=pl.Buffered(1))


def _softplus(x):
    return jnp.maximum(x, 0.0) + jnp.log1p(jnp.exp(-jnp.abs(x)))


def _silu(x):
    return x * jax.nn.sigmoid(x)


def _dot_nt(a, b, precision=None):
    return lax.dot_general(a, b, (((1,), (1,)), ((), ())), precision=precision,
                           preferred_element_type=jnp.float32)


def _dot(a, b, precision=None):
    return jnp.dot(a, b, precision=precision, preferred_element_type=jnp.float32)


def _in_proj_kernel(x_ref, g_ref, wt_ref, gp_ref, qkv_ref, z_ref, fq_ref, fkt_ref, fvt_ref, pu_ref,
                    gt_ref, cum_ref, lft_ref, cumt_ref, carry_ref, *, seg, tm):
    x = x_ref[0]
    h = x * lax.rsqrt(jnp.mean(x * x, axis=-1, keepdims=True) + RMS_EPS) * g_ref[...]
    hb = h.astype(jnp.bfloat16)
    for name, o_ref in (("qkv", qkv_ref), ("z", z_ref), ("fq", fq_ref), ("pu", pu_ref)):
        r0, r1 = seg[name]
        o_ref[0] = _dot_nt(hb, wt_ref[r0:r1, :])
    for name, o_ref in (("fk", fkt_ref), ("fv", fvt_ref)):
        r0, r1 = seg[name]
        o_ref[0] = _dot_nt(wt_ref[r0:r1, :], hb)
    r0, r1 = seg["gates"]
    raw = _dot_nt(hb, wt_ref[r0:r1, :])
    lane = lax.broadcasted_iota(jnp.int32, raw.shape, 1)
    a_log = gp_ref[0:1, :]
    shifted = raw + gp_ref[1:2, :]
    beta = jax.nn.sigmoid(raw)
    g = -jnp.exp(a_log) * _softplus(shifted)
    logf = -_softplus(-shifted)
    gt = jnp.where(lane < GATE_G, beta,
                   jnp.where(lane < GATE_LOGF, g,
                             jnp.where(lane < GATE_LOGF + 6, logf, 0.0)))
    gt_ref[0] = gt

    @pl.when(pl.program_id(1) == 0)
    def _():
        carry_ref[...] = jnp.zeros_like(carry_ref)

    r = lax.broadcasted_iota(jnp.int32, (LANES, LANES), 0)
    c = lax.broadcasted_iota(jnp.int32, (LANES, LANES), 1)
    tril = (c <= r).astype(jnp.float32)
    carry = carry_ref[...]
    for s in range(tm // LANES):
        rows = slice(s * LANES, (s + 1) * LANES)
        cs = _dot(tril, gt[rows], HI) + carry
        cum_ref[0, rows, :] = cs
        cumt_ref[0, :, rows] = cs.T[GATE_LOGF:GATE_LOGF + 8, :]
        lft_ref[0, :, rows] = gt[rows].T[GATE_LOGF:GATE_LOGF + 8, :]
        carry = cs[LANES - 1:LANES, :]
    carry_ref[...] = carry


def _in_proj(x, g_mix, w_t, gate_params, seg, tm):
    b, l, d = x.shape
    row = lambda w: (jax.ShapeDtypeStruct((b, l, w), jnp.float32),
                     pl.BlockSpec((1, tm, w), lambda i, t: (i, t, 0)))
    col = lambda w: (jax.ShapeDtypeStruct((b, w, l), jnp.float32),
                     pl.BlockSpec((1, w, tm), lambda i, t: (i, 0, t)))
    width = lambda name: seg[name][1] - seg[name][0]
    outs = [row(width("qkv")), row(width("z")), row(width("fq")), col(width("fk")), col(width("fv")),
            row(width("pu")), row(LANES), row(LANES), col(8), col(8)]
    return pl.pallas_call(
        functools.partial(_in_proj_kernel, seg=seg, tm=tm),
        grid=(b, l // tm),
        in_specs=[pl.BlockSpec((1, tm, d), lambda i, t: (i, t, 0)), _const_spec((1, d)),
                  _const_spec(w_t.shape), _const_spec((8, LANES))],
        out_specs=[o[1] for o in outs],
        out_shape=[o[0] for o in outs],
        scratch_shapes=[pltpu.VMEM((1, LANES), jnp.float32)],
        compiler_params=_cparams(("arbitrary", "arbitrary")),
    )(x, g_mix, w_t, gate_params)


def _forward_substitute(a_blocks, x_blocks, n):
    for j in range(n - 1):
        for a_sys, x_sys in zip(a_blocks, x_blocks):
            row = jnp.broadcast_to(x_sys[j // 8][j % 8:j % 8 + 1, :], x_sys[0].shape)
            for rr in range((j + 1) // 8, n // 8):
                x_sys[rr] = x_sys[rr] - a_sys[rr][:, j:j + 1] * row
    return x_blocks


def _gdn_kernel(qkv_ref, z_ref, gt_ref, cw_ref, conv0_ref, s0_ref, gn_ref, o_ref, s_ref, hist_ref,
                *, n_heads, valid):
    c = pl.program_id(1)
    C = GDN_CHUNK
    D = HEAD_DIM
    W = n_heads * D

    @pl.when(c == 0)
    def _():
        hist_ref[...] = conv0_ref[0]
        s_ref[...] = s0_ref[...]

    x = qkv_ref[0]
    ext = jnp.concatenate([hist_ref[...], x], axis=0)
    conv = ext[5:5 + C] * cw_ref[0:1, :]
    for j in range(1, 4):
        conv = conv + ext[5 + j:5 + j + C] * cw_ref[j:j + 1, :]
    conv = _silu(conv)
    hist_ref[...] = x[C - 8:C]

    gt = gt_ref[0]
    if valid < C:
        live = lax.broadcasted_iota(jnp.int32, (C, 1), 0) < valid
        conv = jnp.where(live, conv, 0.0)
        gt = jnp.where(live, gt, 0.0)

    ri = lax.broadcasted_iota(jnp.int32, (C, C), 0)
    ci = lax.broadcasted_iota(jnp.int32, (C, C), 1)
    tri = ci <= ri
    strict = ci < ri
    g_cum = _dot(tri.astype(jnp.float32), gt, HI)
    g_cum_t = g_cum.T

    heads, a_blocks, x_blocks = [], [], []
    for h in range(n_heads):
        q = conv[:, h * D:(h + 1) * D]
        k = conv[:, W + h * D:W + (h + 1) * D]
        v = conv[:, 2 * W + h * D:2 * W + (h + 1) * D]
        q = q * lax.rsqrt(jnp.sum(q * q, axis=-1, keepdims=True) + L2_EPS) * (D ** -0.5)
        k = k * lax.rsqrt(jnp.sum(k * k, axis=-1, keepdims=True) + L2_EPS)
        beta = gt[:, GATE_BETA + h:GATE_BETA + h + 1]
        gc = g_cum[:, GATE_G + h:GATE_G + h + 1]
        gr = g_cum_t[GATE_G + h:GATE_G + h + 1, :]
        g_last = gc[C - 1:C, :]
        decay = jnp.where(tri, jnp.exp(jnp.where(tri, gc - gr, 0.0)), 0.0)
        e_g = jnp.exp(gc)
        kb = k * beta
        a_mat = jnp.where(strict, _dot_nt(kb, k) * decay, 0.0)
        rhs = jnp.concatenate([v * beta, kb * e_g], axis=1)
        a_blocks.append([a_mat[r * 8:(r + 1) * 8, :] for r in range(C // 8)])
        x_blocks.append([rhs[r * 8:(r + 1) * 8, :] for r in range(C // 8)])
        heads.append((_dot_nt(q, k) * decay, q * e_g, k * jnp.exp(g_last - gc), g_last))
    x_blocks = _forward_substitute(a_blocks, x_blocks, C)

    outs = []
    for h in range(n_heads):
        qk, qd, kd, g_last = heads[h]
        sol = jnp.concatenate(x_blocks[h], axis=0)
        u = sol[:, :D]
        w = sol[:, D:]
        s_prev = s_ref[0, h]
        v_new = u - _dot(w, s_prev)
        o = _dot(qd, s_prev) + _dot(qk, v_new)
        s_ref[0, h] = s_prev * jnp.exp(g_last) + _dot(kd.T, v_new)
        o = o * lax.rsqrt(jnp.mean(o * o, axis=-1, keepdims=True) + RMS_EPS) * gn_ref[...]
        outs.append(o * _silu(z_ref[0][:, h * D:(h + 1) * D]))
    o_ref[0] = jnp.concatenate(outs, axis=1)


def _gdn(qkv, z, gt, conv_w, conv0, s0, gdn_g, valid):
    b, l, w3 = qkv.shape
    w = w3 // 3
    n_heads = w // HEAD_DIM
    C = GDN_CHUNK
    tok = lambda width: pl.BlockSpec((1, C, width), lambda i, j: (i, j, 0))
    per_b = lambda shape: pl.BlockSpec((1,) + shape, lambda i, j: (i,) + (0,) * len(shape))
    return pl.pallas_call(
        functools.partial(_gdn_kernel, n_heads=n_heads, valid=valid),
        grid=(b, l // C),
        in_specs=[tok(w3), tok(w), tok(LANES), _const_spec((4, w3)), per_b((8, w3)),
                  per_b((n_heads, HEAD_DIM, HEAD_DIM)), _const_spec((1, HEAD_DIM))],
        out_specs=[tok(w), per_b((n_heads, HEAD_DIM, HEAD_DIM))],
        out_shape=[jax.ShapeDtypeStruct((b, l, w), jnp.float32),
                   jax.ShapeDtypeStruct((b, n_heads, HEAD_DIM, HEAD_DIM), jnp.float32)],
        scratch_shapes=[pltpu.VMEM((8, w3), jnp.float32)],
        compiler_params=_cparams(("arbitrary", "arbitrary")),
    )(qkv, z, gt, conv_w, conv0, s0, gdn_g)


def _fox_prompt_kernel(q_ref, kt_ref, vt_ref, cum_ref, cum_t_ref, o_ref, m_ref, l_ref, acc_ref, *, tq):
    hp = pl.program_id(1)
    qi = pl.program_id(2)
    bf = jnp.bfloat16
    lane = lax.broadcasted_iota(jnp.int32, (tq, LANES), 1)
    low = lane < HEAD_DIM
    q = q_ref[0] * (HEAD_DIM ** -0.5)
    q_heads = (jnp.where(low, q, 0.0).astype(bf), jnp.where(low, 0.0, q).astype(bf))
    cum = cum_ref[0]
    head_lane = lax.broadcasted_iota(jnp.int32, cum.shape, 1) - GATE_LOGF - 2 * hp
    c_rows = [jnp.sum(jnp.where(head_lane == hh, cum, 0.0), axis=1, keepdims=True) for hh in range(2)]

    m_ref[...] = jnp.full(m_ref.shape, -jnp.inf, jnp.float32)
    l_ref[...] = jnp.zeros(l_ref.shape, jnp.float32)
    acc_ref[...] = jnp.zeros(acc_ref.shape, jnp.float32)

    def tile(ki, masked):
        ks = pl.multiple_of(ki * tq, tq)
        kt = kt_ref[0, :, pl.ds(ks, tq)].astype(bf)
        vt = vt_ref[0, :, pl.ds(ks, tq)].astype(bf)
        c_cols = cum_t_ref[0, :, pl.ds(ks, tq)]
        sub = lax.broadcasted_iota(jnp.int32, c_cols.shape, 0)
        for hh in range(2):
            c_col = jnp.sum(jnp.where(sub == 2 * hp + hh, c_cols, 0.0), axis=0, keepdims=True)
            s = _dot(q_heads[hh], kt) + c_rows[hh] - c_col
            if masked:
                r = lax.broadcasted_iota(jnp.int32, s.shape, 0)
                cc = lax.broadcasted_iota(jnp.int32, s.shape, 1)
                s = jnp.where(cc <= r, s, -jnp.inf)
            m_prev = m_ref[hh]
            m_new = jnp.maximum(m_prev, jnp.max(s, axis=1, keepdims=True))
            p = jnp.exp(s - jnp.tile(m_new, (1, tq // LANES)))
            alpha = jnp.exp(m_prev - m_new)
            l_ref[hh] = alpha * l_ref[hh] + jnp.sum(p, axis=1, keepdims=True)
            acc_ref[hh] = alpha * acc_ref[hh] + _dot_nt(p.astype(bf), vt)
            m_ref[hh] = m_new

    def body(ki, carry):
        tile(ki, False)
        return carry

    lax.fori_loop(0, qi, body, 0)
    tile(qi, True)
    o_ref[0] = jnp.where(low, acc_ref[0] / l_ref[0], acc_ref[1] / l_ref[1])


def _fox_prompt(fq, fkt, fvt, cum, cum_t, tq):
    b, l, w = fq.shape
    n_pairs = w // LANES
    q_spec = pl.BlockSpec((1, tq, LANES), lambda i, hp, qi: (i, qi, hp))
    kv_spec = pl.BlockSpec((1, LANES, l), lambda i, hp, qi: (i, hp, 0))
    return pl.pallas_call(
        functools.partial(_fox_prompt_kernel, tq=tq),
        grid=(b, n_pairs, l // tq),
        in_specs=[q_spec, kv_spec, kv_spec,
                  pl.BlockSpec((1, tq, LANES), lambda i, hp, qi: (i, qi, 0)),
                  pl.BlockSpec((1, 8, l), lambda i, hp, qi: (i, 0, 0))],
        out_specs=q_spec,
        out_shape=jax.ShapeDtypeStruct((b, l, w), jnp.float32),
        scratch_shapes=[pltpu.VMEM((2, tq, LANES), jnp.float32), pltpu.VMEM((2, tq, LANES), jnp.float32),
                        pltpu.VMEM((2, tq, LANES), jnp.float32)],
        compiler_params=_cparams(("arbitrary", "arbitrary", "arbitrary")),
    )(fq, fkt, fvt, cum, cum_t)


def _fox_sample_kernel(pt_ref, q_ref, kn_ref, vn_ref, gt_ref, *refs, pages_per_step, n_heads):
    del pt_ref
    pp = pages_per_step
    k_refs, v_refs, lf_refs = refs[:pp], refs[pp:2 * pp], refs[2 * pp:3 * pp]
    o_ref, m_ref, l_ref, run_ref, acc_ref = refs[3 * pp:]
    j = pl.program_id(1)
    w = n_heads * HEAD_DIM
    page = k_refs[0].shape[-1]

    sub = lax.broadcasted_iota(jnp.int32, (8, w), 0)
    own = lax.broadcasted_iota(jnp.int32, (8, w), 1) // HEAD_DIM == sub
    qb = jnp.where(own, q_ref[0] * (HEAD_DIM ** -0.5), 0.0)

    @pl.when(j == 0)
    def _():
        m_ref[...] = jnp.sum(qb * kn_ref[0], axis=1, keepdims=True)
        l_ref[...] = jnp.ones_like(l_ref)
        acc_ref[...] = jnp.broadcast_to(vn_ref[0], acc_ref.shape)
        gt = gt_ref[0]
        pick = (lax.broadcasted_iota(jnp.int32, (8, LANES), 1) - GATE_LOGF
                == lax.broadcasted_iota(jnp.int32, (8, LANES), 0))
        run_ref[...] = jnp.sum(jnp.where(pick, gt, 0.0), axis=1, keepdims=True)

    r = lax.broadcasted_iota(jnp.int32, (page, page), 0)
    c = lax.broadcasted_iota(jnp.int32, (page, page), 1)
    later = (r > c).astype(jnp.float32)
    qbb = qb.astype(jnp.bfloat16)
    run = run_ref[...]
    scores = []
    for i in range(pp):
        lf = jnp.concatenate([lf_refs[i][0, 0], jnp.zeros((8 - n_heads, page), jnp.float32)], axis=0)
        s = _dot(qbb, k_refs[i][0, 0].astype(jnp.bfloat16))
        scores.append(s + run + _dot(lf, later, HI))
        run = run + jnp.sum(lf, axis=1, keepdims=True)
    run_ref[...] = run
    s_all = jnp.concatenate(scores, axis=1)
    m_prev = m_ref[...]
    m_new = jnp.maximum(m_prev, jnp.max(s_all, axis=1, keepdims=True))
    p = jnp.exp(s_all - m_new)
    alpha = jnp.exp(m_prev - m_new)
    l_ref[...] = alpha * l_ref[...] + jnp.sum(p, axis=1, keepdims=True)
    pv = _dot_nt(p[:, :page].astype(jnp.bfloat16), v_refs[0][0, 0].astype(jnp.bfloat16))
    for i in range(1, pp):
        pv = pv + _dot_nt(p[:, i * page:(i + 1) * page].astype(jnp.bfloat16),
                          v_refs[i][0, 0].astype(jnp.bfloat16))
    acc_ref[...] = alpha * acc_ref[...] + pv
    m_ref[...] = m_new

    @pl.when(j == pl.num_programs(1) - 1)
    def _():
        o = jnp.where(own, acc_ref[...] / l_ref[...], 0.0)
        o_ref[0] = jnp.sum(o, axis=0, keepdims=True)


def _fox_sample(page_table, fq, fk, fv, gt, cache_kt, cache_vt, cache_lf_t, layer, pages_per_step):
    b, _, w = fq.shape
    n_heads = w // HEAD_DIM
    page = cache_kt.shape[-1]
    n_pages = page_table.shape[1]
    pp = pages_per_step

    def page_map(i):
        return lambda bi, j, pt: (layer, pt[bi, n_pages - 1 - (j * pp + i)], 0, 0)

    tok = lambda width: pl.BlockSpec((1, 1, width), lambda bi, j, pt: (bi, 0, 0))
    in_specs = [tok(w), tok(w), tok(w), tok(LANES)]
    in_specs += [pl.BlockSpec((1, 1, w, page), page_map(i)) for i in range(pp)]
    in_specs += [pl.BlockSpec((1, 1, w, page), page_map(i)) for i in range(pp)]
    in_specs += [pl.BlockSpec((1, 1, n_heads, page), page_map(i)) for i in range(pp)]
    grid_spec = pltpu.PrefetchScalarGridSpec(
        num_scalar_prefetch=1, grid=(b, n_pages // pp), in_specs=in_specs, out_specs=tok(w),
        scratch_shapes=[pltpu.VMEM((8, 1), jnp.float32), pltpu.VMEM((8, 1), jnp.float32),
                        pltpu.VMEM((8, 1), jnp.float32), pltpu.VMEM((8, w), jnp.float32)])
    return pl.pallas_call(
        functools.partial(_fox_sample_kernel, pages_per_step=pp, n_heads=n_heads),
        grid_spec=grid_spec,
        out_shape=jax.ShapeDtypeStruct((b, 1, w), jnp.float32),
        compiler_params=_cparams(("arbitrary", "arbitrary")),
    )(page_table, fq, fk, fv, gt, *([cache_kt] * pp), *([cache_vt] * pp), *([cache_lf_t] * pp))


def _pool_kernel(u_ref, hist0_ref, w_ref, scale_ref, o_ref, hist_ref, *, pos0, tl):
    t = pl.program_id(1)

    @pl.when(t == 0)
    def _():
        hist_ref[...] = hist0_ref[0]

    u = u_ref[0]
    ext = jnp.concatenate([hist_ref[...], u], axis=0)
    hist_ref[...] = ext[tl:tl + POOL_HIST]
    sums = []
    acc = ext
    for step in (1, 2, 4, 8):
        acc = acc + pltpu.roll(acc, step, axis=0)
        sums.append(acc[POOL_HIST:])
    width = u.shape[1]
    group = lax.broadcasted_iota(jnp.int32, (tl, width), 1) // (width // len(POOL_WINDOWS))
    wsum = jnp.where(group == 0, sums[0], jnp.where(group == 1, sums[1],
                                                    jnp.where(group == 2, sums[2], sums[3])))
    window = jnp.where(group == 0, POOL_WINDOWS[0],
                       jnp.where(group == 1, POOL_WINDOWS[1],
                                 jnp.where(group == 2, POOL_WINDOWS[2], POOL_WINDOWS[3])))
    pos = pos0 + t * tl + lax.broadcasted_iota(jnp.int32, (tl, width), 0)
    cnt = jnp.minimum(pos + 1, window).astype(jnp.float32)
    d = wsum / cnt - u
    o_ref[0] = _dot(d.astype(jnp.bfloat16), w_ref[...]) * scale_ref[...]


def _pool(u, hist0, w_bd, scale, pos0, tl):
    b, l, w = u.shape
    return pl.pallas_call(
        functools.partial(_pool_kernel, pos0=pos0, tl=tl),
        grid=(b, l // tl),
        in_specs=[pl.BlockSpec((1, tl, w), lambda i, t: (i, t, 0)),
                  pl.BlockSpec((1, POOL_HIST, w), lambda i, t: (i, 0, 0)),
                  _const_spec((w, w)), _const_spec((1, w))],
        out_specs=pl.BlockSpec((1, tl, w), lambda i, t: (i, t, 0)),
        out_shape=jax.ShapeDtypeStruct((b, l, w), jnp.float32),
        scratch_shapes=[pltpu.VMEM((POOL_HIST, w), jnp.float32)],
        compiler_params=_cparams(("arbitrary", "arbitrary")),
    )(u, hist0, w_bd, scale)


def _tail_kernel(x_ref, oa_ref, of_ref, oc_ref, wo_ref, gf_ref, wg_ref, wu_ref, wd_ref, gl_ref, o_ref,
                 *, ff_chunks, final_norm):
    bf = jnp.bfloat16
    mixed = jnp.concatenate([oa_ref[...], of_ref[...], oc_ref[...]], axis=1).astype(bf)
    x = x_ref[...] + _dot(mixed, wo_ref[...])
    h = (x * lax.rsqrt(jnp.mean(x * x, axis=-1, keepdims=True) + RMS_EPS) * gf_ref[...]).astype(bf)
    for c0, c1 in ff_chunks:
        act = _silu(_dot(h, wg_ref[:, c0:c1])) * _dot(h, wu_ref[:, c0:c1])
        x = x + _dot(act.astype(bf), wd_ref[c0:c1, :])
    if final_norm:
        x = x * lax.rsqrt(jnp.mean(x * x, axis=-1, keepdims=True) + RMS_EPS) * gl_ref[...]
    o_ref[...] = x


def _tail(x2d, oa, of, oc, wo, g_ffn, wg, wu, wd, g_last, final_norm, tm):
    n, d = x2d.shape
    d_ff = wg.shape[1]
    step = 1024
    ff_chunks = tuple((c0, min(c0 + step, d_ff)) for c0 in range(0, d_ff, step))
    row_spec = lambda w: pl.BlockSpec((tm, w), lambda i: (i, 0))
    weights = (wo, g_ffn, wg, wu, wd, g_last)
    return pl.pallas_call(
        functools.partial(_tail_kernel, ff_chunks=ff_chunks, final_norm=final_norm),
        grid=(n // tm,),
        in_specs=[row_spec(d), row_spec(oa.shape[1]), row_spec(of.shape[1]), row_spec(oc.shape[1])]
        + [_const_spec(a.shape) for a in weights],
        out_specs=row_spec(d),
        out_shape=jax.ShapeDtypeStruct((n, d), jnp.float32),
        compiler_params=_cparams(("arbitrary",)),
    )(x2d, oa, of, oc, *weights)


def _layer_weights(l, w_gdn, w_fox, w_pool, norm_mix_g, w_in_t, conv_w, a_log, dt_bias, gdn_norm_g,
                   fox_bf, pool_w, pool_scale, w_out, norm_ffn_g, w_gate_up, w_down):
    bf = jnp.bfloat16
    n_hg = w_gdn // HEAD_DIM
    n_hf = w_fox // HEAD_DIM
    d_ff = w_down.shape[1]
    o = 0
    src = {}
    for name, width in (("qkv", 3 * w_gdn), ("z", w_gdn), ("beta", n_hg), ("alpha", n_hg),
                        ("fq", w_fox), ("fk", w_fox), ("fv", w_fox), ("fgate", n_hf), ("pu", w_pool)):
        src[name] = (o, o + width)
        o += width
    wl = w_in_t[:, l, :]
    rows = lambda name: wl[src[name][0]:src[name][1]]
    gate_rows = jnp.concatenate([rows("beta"), rows("alpha"), rows("fgate")], axis=0)
    gate_rows = jnp.pad(gate_rows, ((0, LANES - gate_rows.shape[0]), (0, 0)))
    order = ("qkv", "z", "fq", "fk", "fv", "pu")
    w_t = jnp.concatenate([rows(n) for n in order] + [gate_rows], axis=0).astype(bf)
    seg, o = {}, 0
    for name in order:
        width = src[name][1] - src[name][0]
        seg[name] = (o, o + width)
        o += width
    seg["gates"] = (o, o + LANES)
    gp = jnp.zeros((8, LANES), jnp.float32)
    gp = gp.at[0, GATE_G:GATE_G + n_hg].set(a_log[l])
    gp = gp.at[1, GATE_G:GATE_G + n_hg].set(dt_bias[l])
    gp = gp.at[1, GATE_LOGF:GATE_LOGF + n_hf].set(fox_bf[l])
    n_groups, pg, _ = pool_w[l].shape
    w_bd = jnp.zeros((w_pool, w_pool), jnp.float32)
    for gi in range(n_groups):
        w_bd = w_bd.at[gi * pg:(gi + 1) * pg, gi * pg:(gi + 1) * pg].set(pool_w[l, gi])
    return dict(
        g_mix=norm_mix_g[l][None, :], w_t=w_t, seg=seg, gp=gp, conv_w=conv_w[l],
        gdn_g=gdn_norm_g[l][None, :], w_bd=w_bd.astype(bf), pool_scale=pool_scale[l][None, :],
        wo=w_out[l].astype(bf),
        g_ffn=norm_ffn_g[l][None, :], wg=w_gate_up[l][:, :d_ff].astype(bf),
        wu=w_gate_up[l][:, d_ff:].astype(bf), wd=w_down[l].astype(bf))


def kernel(x_prompt, x_sample, cache_k, cache_v, cache_logf, state_gdn, state_conv, state_pool,
           page_table, norm_mix_g, w_in, conv_w, a_log, dt_bias, gdn_norm_g, fox_bf, pool_w,
           pool_scale, w_out, norm_ffn_g, w_gate_up, w_down, final_norm_g):
    f32 = jnp.float32
    bp, seq, d = x_prompt.shape
    bs, dec_seq, _ = x_sample.shape
    depth, n_pool, page, n_hf, _ = cache_k.shape
    n_hg = state_gdn.shape[2]
    w_gdn = n_hg * HEAD_DIM
    w_fox = n_hf * HEAD_DIM
    w_pool = state_pool.shape[-1]
    pool_buf = state_pool.shape[2]
    conv_hist = state_conv.shape[2]
    past_len = page_table.shape[1] * page
    C = GDN_CHUNK
    assert dec_seq == 1 and seq % LANES == 0 and conv_hist == 3 and pool_buf == POOL_HIST - 1
    assert bs <= LANES

    tm_p = min(512, seq)
    tq = min(512, seq)
    tl = min(512, seq)
    pps = 8 if page_table.shape[1] % 8 == 0 else 1
    g_last = final_norm_g[None, :]

    xp = x_prompt.reshape(bp * seq, d)
    xs = x_sample.reshape(bs, d)
    cache_kt = jnp.transpose(cache_k, (0, 1, 3, 4, 2)).reshape(depth, n_pool, w_fox, page)
    cache_vt = jnp.transpose(cache_v, (0, 1, 3, 4, 2)).reshape(depth, n_pool, w_fox, page)
    cache_lf_t = jnp.swapaxes(cache_logf, 2, 3)
    w_in_t = jnp.transpose(w_in, (2, 0, 1))

    st_p, st_s = [], []
    for l in range(depth):
        lw = _layer_weights(l, w_gdn, w_fox, w_pool, norm_mix_g, w_in_t, conv_w, a_log, dt_bias,
                            gdn_norm_g, fox_bf, pool_w, pool_scale, w_out, norm_ffn_g, w_gate_up, w_down)
        final = l == depth - 1

        qkv, z, fq, fkt, fvt, pu, gt, cum, lft, cumt = _in_proj(
            xp.reshape(bp, seq, d), lw["g_mix"], lw["w_t"], lw["gp"], lw["seg"], tm_p)
        o_a, s_fin = _gdn(qkv, z, gt, lw["conv_w"], jnp.zeros((bp, 8, 3 * w_gdn), f32),
                          jnp.zeros((bp, n_hg, HEAD_DIM, HEAD_DIM), f32), lw["gdn_g"], C)
        o_f = _fox_prompt(fq, fkt, fvt, cum, cumt, tq)
        o_c = _pool(pu, jnp.zeros((bp, POOL_HIST, w_pool), f32), lw["w_bd"], lw["pool_scale"], 0, tl)
        xp = _tail(xp, o_a.reshape(bp * seq, w_gdn), o_f.reshape(bp * seq, w_fox),
                   o_c.reshape(bp * seq, w_pool), lw["wo"], lw["g_ffn"],
                   lw["wg"], lw["wu"], lw["wd"], g_last, final, tm_p if tm_p <= 256 else 256)
        heads_last = lambda a: jnp.transpose(a.reshape(bp, n_hf, HEAD_DIM, seq), (0, 3, 1, 2))
        st_p.append((heads_last(fkt), heads_last(fvt), jnp.swapaxes(lft[:, :n_hf, :], 1, 2), s_fin,
                     qkv[:, seq - conv_hist:], pu[:, seq - pool_buf:]))

        xs_rows = jnp.pad(xs, ((0, LANES - bs), (0, 0)))[None]
        qkv, z, fq, fkt, fvt, pu, gt, _, _, _ = _in_proj(xs_rows, lw["g_mix"], lw["w_t"], lw["gp"],
                                                         lw["seg"], LANES)
        qkv, z, fq, pu, gt = (a[0, :bs] for a in (qkv, z, fq, pu, gt))
        fk, fv = (jnp.swapaxes(a[0, :, :bs], 0, 1) for a in (fkt, fvt))
        pad_c = lambda a: jnp.pad(a[:, None, :], ((0, 0), (0, C - 1), (0, 0)))
        conv0 = jnp.pad(state_conv[l], ((0, 0), (8 - conv_hist, 0), (0, 0)))
        o_a, s_fin = _gdn(pad_c(qkv), pad_c(z), pad_c(gt), lw["conv_w"], conv0, state_gdn[l],
                          lw["gdn_g"], 1)
        o_f = _fox_sample(page_table, fq[:, None, :], fk[:, None, :], fv[:, None, :], gt[:, None, :],
                          cache_kt, cache_vt, cache_lf_t, l, pps)
        hist0 = jnp.pad(state_pool[l], ((0, 0), (POOL_HIST - pool_buf, 0), (0, 0)))
        o_c = _pool(jnp.pad(pu[:, None, :], ((0, 0), (0, 7), (0, 0))), hist0, lw["w_bd"],
                    lw["pool_scale"], past_len, 8)
        xs = _tail(xs, o_a[:, 0], o_f[:, 0], o_c[:, 0], lw["wo"], lw["g_ffn"],
                   lw["wg"], lw["wu"], lw["wd"], g_last, final, bs)
        st_s.append((fk.reshape(bs, 1, n_hf, HEAD_DIM), fv.reshape(bs, 1, n_hf, HEAD_DIM),
                     gt[:, None, GATE_LOGF:GATE_LOGF + n_hf], s_fin,
                     jnp.concatenate([state_conv[l][:, 1:], qkv[:, None, :]], axis=1),
                     jnp.concatenate([state_pool[l][:, 1:], pu[:, None, :]], axis=1)))

    outs = [xp.reshape(bp, seq, d), xs.reshape(bs, 1, d)]
    for st in (st_p, st_s):
        for i in range(6):
            outs.append(jnp.stack([s[i] for s in st]))
    return tuple(outs)
```

```python
import functools

import jax
import jax.numpy as jnp
from jax import lax
from jax.experimental import pallas as pl
from jax.experimental.pallas import tpu as pltpu

HEAD_DIM = 64
POOL_WINDOWS = (2, 4, 8, 16)
POOL_HIST = 16
GDN_CHUNK = 64
SOLVE_BLOCK = 8
RMS_EPS = 1e-6
L2_EPS = 1e-6
LANES = 128
VMEM_LIMIT = 56 * 1024 * 1024
HI = lax.Precision.HIGHEST

GATE_BETA = 0
GATE_G = 6
GATE_LOGF = 12


def _cparams(sem):
    return pltpu.CompilerParams(dimension_semantics=sem, vmem_limit_bytes=VMEM_LIMIT)


def _const_spec(shape):
    nd = len(shape)
    return pl.BlockSpec(shape, lambda *_: (0,) * nd, pipeline_mode=pl.Buffered(1))


def _softplus(x):
    return jnp.maximum(x, 0.0) + jnp.log1p(jnp.exp(-jnp.abs(x)))


def _silu(x):
    return x * jax.nn.sigmoid(x)


def _dot_nt(a, b, precision=None):
    return lax.dot_general(a, b, (((1,), (1,)), ((), ())), precision=precision,
                           preferred_element_type=jnp.float32)


def _dot(a, b, precision=None):
    return jnp.dot(a, b, precision=precision, preferred_element_type=jnp.float32)


def _in_proj_kernel(x_ref, g_ref, wt_ref, gp_ref, qkv_ref, z_ref, fq_ref, fkt_ref, fvt_ref, pu_ref,
                    gt_ref, cum_ref, lft_ref, cumt_ref, carry_ref, *, seg, tm):
    x = x_ref[0]
    h = x * lax.rsqrt(jnp.mean(x * x, axis=-1, keepdims=True) + RMS_EPS) * g_ref[...]
    hb = h.astype(jnp.bfloat16)
    for name, o_ref in (("qkv", qkv_ref), ("z", z_ref), ("fq", fq_ref), ("pu", pu_ref)):
        r0, r1 = seg[name]
        o_ref[0] = _dot_nt(hb, wt_ref[r0:r1, :])
    for name, o_ref in (("fk", fkt_ref), ("fv", fvt_ref)):
        r0, r1 = seg[name]
        o_ref[0] = _dot_nt(wt_ref[r0:r1, :], hb)
    r0, r1 = seg["gates"]
    raw = _dot_nt(hb, wt_ref[r0:r1, :])
    lane = lax.broadcasted_iota(jnp.int32, raw.shape, 1)
    a_log = gp_ref[0:1, :]
    shifted = raw + gp_ref[1:2, :]
    beta = jax.nn.sigmoid(raw)
    g = -jnp.exp(a_log) * _softplus(shifted)
    logf = -_softplus(-shifted)
    gt = jnp.where(lane < GATE_G, beta,
                   jnp.where(lane < GATE_LOGF, g,
                             jnp.where(lane < GATE_LOGF + 6, logf, 0.0)))
    gt_ref[0] = gt

    @pl.when(pl.program_id(1) == 0)
    def _():
        carry_ref[...] = jnp.zeros_like(carry_ref)

    r = lax.broadcasted_iota(jnp.int32, (LANES, LANES), 0)
    c = lax.broadcasted_iota(jnp.int32, (LANES, LANES), 1)
    tril = (c <= r).astype(jnp.float32)
    carry = carry_ref[...]
    for s in range(tm // LANES):
        rows = slice(s * LANES, (s + 1) * LANES)
        cs = _dot(tril, gt[rows], HI) + carry
        cum_ref[0, rows, :] = cs
        cumt_ref[0, :, rows] = cs.T[GATE_LOGF:GATE_LOGF + 8, :]
        lft_ref[0, :, rows] = gt[rows].T[GATE_LOGF:GATE_LOGF + 8, :]
        carry = cs[LANES - 1:LANES, :]
    carry_ref[...] = carry


def _in_proj(x, g_mix, w_t, gate_params, seg, tm):
    b, l, d = x.shape
    row = lambda w: (jax.ShapeDtypeStruct((b, l, w), jnp.float32),
                     pl.BlockSpec((1, tm, w), lambda i, t: (i, t, 0)))
    col = lambda w: (jax.ShapeDtypeStruct((b, w, l), jnp.float32),
                     pl.BlockSpec((1, w, tm), lambda i, t: (i, 0, t)))
    width = lambda name: seg[name][1] - seg[name][0]
    outs = [row(width("qkv")), row(width("z")), row(width("fq")), col(width("fk")), col(width("fv")),
            row(width("pu")), row(LANES), row(LANES), col(8), col(8)]
    return pl.pallas_call(
        functools.partial(_in_proj_kernel, seg=seg, tm=tm),
        grid=(b, l // tm),
        in_specs=[pl.BlockSpec((1, tm, d), lambda i, t: (i, t, 0)), _const_spec((1, d)),
                  _const_spec(w_t.shape), _const_spec((8, LANES))],
        out_specs=[o[1] for o in outs],
        out_shape=[o[0] for o in outs],
        scratch_shapes=[pltpu.VMEM((1, LANES), jnp.float32)],
        compiler_params=_cparams(("arbitrary", "arbitrary")),
    )(x, g_mix, w_t, gate_params)


def _forward_substitute(a_blocks, x_blocks, lo, hi):
    for j in range(lo, hi - 1):
        for a_sys, x_sys in zip(a_blocks, x_blocks):
            row = jnp.broadcast_to(x_sys[j // 8][j % 8:j % 8 + 1, :], x_sys[0].shape)
            for rr in range((j + 1) // 8, hi // 8):
                x_sys[rr] = x_sys[rr] - a_sys[rr][:, j:j + 1] * row


def _solve_unit_lower(a_mats, rhs, n):
    bs = SOLVE_BLOCK
    a_blocks = [[a[r * 8:(r + 1) * 8, :] for r in range(n // 8)] for a in a_mats]
    x_blocks = [[x[r * 8:(r + 1) * 8, :] for r in range(n // 8)] for x in rhs]
    for lo in range(0, n, bs):
        if lo > 0:
            for a, x_sys in zip(a_mats, x_blocks):
                done = jnp.concatenate(x_sys[:lo // 8], axis=0)
                cur = jnp.concatenate(x_sys[lo // 8:(lo + bs) // 8], axis=0)
                cur = cur - _dot(a[lo:lo + bs, :lo], done, HI)
                x_sys[lo // 8:(lo + bs) // 8] = [cur[r * 8:(r + 1) * 8, :] for r in range(bs // 8)]
        _forward_substitute(a_blocks, x_blocks, lo, lo + bs)
    return [jnp.concatenate(x_sys, axis=0) for x_sys in x_blocks]


def _gdn_prep_kernel(qkv_ref, gt_ref, cw_ref, conv0_ref, mats_ref, qd_ref, glast_ref, hist_ref,
                     *, n_heads, valid, chunks):
    c = pl.program_id(1)
    C = GDN_CHUNK
    D = HEAD_DIM
    W = n_heads * D

    @pl.when(c == 0)
    def _():
        hist_ref[...] = conv0_ref[0]

    x = qkv_ref[0]
    rows = chunks * C
    ext = jnp.concatenate([hist_ref[...], x], axis=0)
    conv_all = ext[5:5 + rows] * cw_ref[0:1, :]
    for j in range(1, 4):
        conv_all = conv_all + ext[5 + j:5 + j + rows] * cw_ref[j:j + 1, :]
    conv_all = _silu(conv_all)
    hist_ref[...] = x[rows - 8:rows]

    ri = lax.broadcasted_iota(jnp.int32, (C, C), 0)
    ci = lax.broadcasted_iota(jnp.int32, (C, C), 1)
    tri = ci <= ri
    strict = ci < ri

    pending = []
    for n in range(chunks):
        conv = conv_all[n * C:(n + 1) * C]
        gt = gt_ref[0, n * C:(n + 1) * C, :]
        if valid < C:
            live = lax.broadcasted_iota(jnp.int32, (C, 1), 0) < valid
            conv = jnp.where(live, conv, 0.0)
            gt = jnp.where(live, gt, 0.0)
        g_cum = _dot(tri.astype(jnp.float32), gt, HI)
        g_cum_t = g_cum.T
        glast_ref[0, n] = jnp.broadcast_to(g_cum[C - 1:C, :], (8, LANES))
        a_mats, rhs_all, side = [], [], []
        for h in range(n_heads):
            q = conv[:, h * D:(h + 1) * D]
            k = conv[:, W + h * D:W + (h + 1) * D]
            v = conv[:, 2 * W + h * D:2 * W + (h + 1) * D]
            q = q * lax.rsqrt(jnp.sum(q * q, axis=-1, keepdims=True) + L2_EPS) * (D ** -0.5)
            k = k * lax.rsqrt(jnp.sum(k * k, axis=-1, keepdims=True) + L2_EPS)
            beta = gt[:, GATE_BETA + h:GATE_BETA + h + 1]
            gc = g_cum[:, GATE_G + h:GATE_G + h + 1]
            gr = g_cum_t[GATE_G + h:GATE_G + h + 1, :]
            g_last = gc[C - 1:C, :]
            decay = jnp.where(tri, jnp.exp(jnp.where(tri, gc - gr, 0.0)), 0.0)
            e_g = jnp.exp(gc)
            kb = k * beta
            a_mats.append(jnp.where(strict, _dot_nt(kb, k) * decay, 0.0))
            rhs_all.append(jnp.concatenate([v * beta, kb * e_g], axis=1))
            side.append((_dot_nt(q, k) * decay, q * e_g, (k * jnp.exp(g_last - gc)).T))
        pending.append((a_mats, rhs_all, side))

    for n, (a_mats, rhs_all, side) in enumerate(pending):
        for h, sol in enumerate(_solve_unit_lower(a_mats, rhs_all, C)):
            qk, qd, kd_t = side[h]
            mats_ref[0, n, h, 0:C, :] = _dot(qk, sol)
            mats_ref[0, n, h, C:2 * C, :] = _dot(kd_t, sol)
            qd_ref[0, n, h] = qd


def _gdn_scan_kernel(mats_ref, qd_ref, glast_ref, z_ref, s0_ref, gn_ref, o_ref, s_ref,
                     *, n_heads, chunks):
    C = GDN_CHUNK
    D = HEAD_DIM

    @pl.when(pl.program_id(1) == 0)
    def _():
        s_ref[...] = s0_ref[...]

    for n in range(chunks):
        outs = []
        for h in range(n_heads):
            g_last = glast_ref[0, n][0:1, GATE_G + h:GATE_G + h + 1]
            s_prev = s_ref[0, h]
            s_low = jnp.concatenate([jnp.zeros((D, D), jnp.float32), s_prev], axis=0)
            moved = _dot(mats_ref[0, n, h], s_low)
            o = _dot(qd_ref[0, n, h], s_prev) - moved[:C] + mats_ref[0, n, h, 0:C, 0:D]
            s_ref[0, h] = s_prev * jnp.exp(g_last) - moved[C:] + mats_ref[0, n, h, C:2 * C, 0:D]
            o = o * lax.rsqrt(jnp.mean(o * o, axis=-1, keepdims=True) + RMS_EPS) * gn_ref[...]
            outs.append(o * _silu(z_ref[0, n * C:(n + 1) * C, h * D:(h + 1) * D]))
        o_ref[0, n * C:(n + 1) * C, :] = jnp.concatenate(outs, axis=1)


def _gdn(qkv, z, gt, conv_w, conv0, s0, gdn_g, valid):
    b, l, w3 = qkv.shape
    w = w3 // 3
    n_heads = w // HEAD_DIM
    C = GDN_CHUNK
    D = HEAD_DIM
    nc = l // C
    f32 = jnp.float32
    per_b = lambda shape: pl.BlockSpec((1,) + shape, lambda i, j: (i,) + (0,) * len(shape))
    chunk = lambda n, shape: pl.BlockSpec((1, n) + shape, lambda i, j: (i, j) + (0,) * len(shape))
    tok = lambda n, width: pl.BlockSpec((1, n * C, width), lambda i, j: (i, j, 0))
    cpp = max(n for n in (2, 1) if nc % n == 0)
    mats, qd, glast = pl.pallas_call(
        functools.partial(_gdn_prep_kernel, n_heads=n_heads, valid=valid, chunks=cpp),
        grid=(b, nc // cpp),
        in_specs=[tok(cpp, w3), tok(cpp, LANES), _const_spec((4, w3)), per_b((8, w3))],
        out_specs=[chunk(cpp, (n_heads, 2 * C, 2 * D)), chunk(cpp, (n_heads, C, D)), chunk(cpp, (8, LANES))],
        out_shape=[jax.ShapeDtypeStruct((b, nc, n_heads, 2 * C, 2 * D), f32),
                   jax.ShapeDtypeStruct((b, nc, n_heads, C, D), f32),
                   jax.ShapeDtypeStruct((b, nc, 8, LANES), f32)],
        scratch_shapes=[pltpu.VMEM((8, w3), f32)],
        compiler_params=_cparams(("arbitrary", "arbitrary")),
    )(qkv, gt, conv_w, conv0)
    cps = max(n for n in (4, 2, 1) if nc % n == 0)
    return pl.pallas_call(
        functools.partial(_gdn_scan_kernel, n_heads=n_heads, chunks=cps),
        grid=(b, nc // cps),
        in_specs=[chunk(cps, (n_heads, 2 * C, 2 * D)), chunk(cps, (n_heads, C, D)), chunk(cps, (8, LANES)),
                  tok(cps, w), per_b((n_heads, D, D)), _const_spec((1, D))],
        out_specs=[tok(cps, w), per_b((n_heads, D, D))],
        out_shape=[jax.ShapeDtypeStruct((b, l, w), f32), jax.ShapeDtypeStruct((b, n_heads, D, D), f32)],
        compiler_params=_cparams(("arbitrary", "arbitrary")),
    )(mats, qd, glast, z, s0, gdn_g)


def _fox_prompt_kernel(q_ref, kt_ref, vt_ref, cum_ref, cum_t_ref, o_ref, m_ref, l_ref, acc_ref, *, tq):
    hp = pl.program_id(1)
    qi = pl.program_id(2)
    bf = jnp.bfloat16
    lane = lax.broadcasted_iota(jnp.int32, (tq, LANES), 1)
    low = lane < HEAD_DIM
    q = q_ref[0] * (HEAD_DIM ** -0.5)
    q_heads = (jnp.where(low, q, 0.0).astype(bf), jnp.where(low, 0.0, q).astype(bf))
    cum = cum_ref[0]
    head_lane = lax.broadcasted_iota(jnp.int32, cum.shape, 1) - GATE_LOGF - 2 * hp
    c_rows = [jnp.sum(jnp.where(head_lane == hh, cum, 0.0), axis=1, keepdims=True) for hh in range(2)]

    m_ref[...] = jnp.full(m_ref.shape, -jnp.inf, jnp.float32)
    l_ref[...] = jnp.zeros(l_ref.shape, jnp.float32)
    acc_ref[...] = jnp.zeros(acc_ref.shape, jnp.float32)

    def tile(ki, masked):
        ks = pl.multiple_of(ki * tq, tq)
        kt = kt_ref[0, :, pl.ds(ks, tq)].astype(bf)
        vt = vt_ref[0, :, pl.ds(ks, tq)].astype(bf)
        c_cols = cum_t_ref[0, :, pl.ds(ks, tq)]
        sub = lax.broadcasted_iota(jnp.int32, c_cols.shape, 0)
        for hh in range(2):
            c_col = jnp.sum(jnp.where(sub == 2 * hp + hh, c_cols, 0.0), axis=0, keepdims=True)
            s = _dot(q_heads[hh], kt) + c_rows[hh] - c_col
            if masked:
                r = lax.broadcasted_iota(jnp.int32, s.shape, 0)
                cc = lax.broadcasted_iota(jnp.int32, s.shape, 1)
                s = jnp.where(cc <= r, s, -jnp.inf)
            m_prev = m_ref[hh]
            m_new = jnp.maximum(m_prev, jnp.max(s, axis=1, keepdims=True))
            p = jnp.exp(s - jnp.tile(m_new, (1, tq // LANES)))
            alpha = jnp.exp(m_prev - m_new)
            l_ref[hh] = alpha * l_ref[hh] + jnp.sum(p, axis=1, keepdims=True)
            acc_ref[hh] = alpha * acc_ref[hh] + _dot_nt(p.astype(bf), vt)
            m_ref[hh] = m_new

    def body(ki, carry):
        tile(ki, False)
        return carry

    lax.fori_loop(0, qi, body, 0)
    tile(qi, True)
    o_ref[0] = jnp.where(low, acc_ref[0] / l_ref[0], acc_ref[1] / l_ref[1])


def _fox_prompt(fq, fkt, fvt, cum, cum_t, tq):
    b, l, w = fq.shape
    n_pairs = w // LANES
    q_spec = pl.BlockSpec((1, tq, LANES), lambda i, hp, qi: (i, qi, hp))
    kv_spec = pl.BlockSpec((1, LANES, l), lambda i, hp, qi: (i, hp, 0))
    return pl.pallas_call(
        functools.partial(_fox_prompt_kernel, tq=tq),
        grid=(b, n_pairs, l // tq),
        in_specs=[q_spec, kv_spec, kv_spec,
                  pl.BlockSpec((1, tq, LANES), lambda i, hp, qi: (i, qi, 0)),
                  pl.BlockSpec((1, 8, l), lambda i, hp, qi: (i, 0, 0))],
        out_specs=q_spec,
        out_shape=jax.ShapeDtypeStruct((b, l, w), jnp.float32),
        scratch_shapes=[pltpu.VMEM((2, tq, LANES), jnp.float32), pltpu.VMEM((2, tq, LANES), jnp.float32),
                        pltpu.VMEM((2, tq, LANES), jnp.float32)],
        compiler_params=_cparams(("arbitrary", "arbitrary", "arbitrary")),
    )(fq, fkt, fvt, cum, cum_t)


def _fox_sample_kernel(pt_ref, q_ref, kn_ref, vn_ref, gt_ref, *refs, pages_per_step, n_heads):
    del pt_ref
    pp = pages_per_step
    k_refs, v_refs, lf_refs = refs[:pp], refs[pp:2 * pp], refs[2 * pp:3 * pp]
    o_ref, m_ref, l_ref, run_ref, acc_ref = refs[3 * pp:]
    j = pl.program_id(1)
    w = n_heads * HEAD_DIM
    page = k_refs[0].shape[-1]

    sub = lax.broadcasted_iota(jnp.int32, (8, w), 0)
    own = lax.broadcasted_iota(jnp.int32, (8, w), 1) // HEAD_DIM == sub
    qb = jnp.where(own, q_ref[0] * (HEAD_DIM ** -0.5), 0.0)

    @pl.when(j == 0)
    def _():
        m_ref[...] = jnp.sum(qb * kn_ref[0], axis=1, keepdims=True)
        l_ref[...] = jnp.ones_like(l_ref)
        acc_ref[...] = jnp.broadcast_to(vn_ref[0], acc_ref.shape)
        gt = gt_ref[0]
        pick = (lax.broadcasted_iota(jnp.int32, (8, LANES), 1) - GATE_LOGF
                == lax.broadcasted_iota(jnp.int32, (8, LANES), 0))
        run_ref[...] = jnp.sum(jnp.where(pick, gt, 0.0), axis=1, keepdims=True)

    r = lax.broadcasted_iota(jnp.int32, (page, page), 0)
    c = lax.broadcasted_iota(jnp.int32, (page, page), 1)
    later = (r > c).astype(jnp.float32)
    qbb = qb.astype(jnp.bfloat16)
    run = run_ref[...]
    scores = []
    pad = jnp.zeros((8 - n_heads, page), jnp.float32)
    lf_all = jnp.concatenate([x for i in range(pp) for x in (lf_refs[i][0, 0], pad)], axis=0)
    suffix = _dot(lf_all, later, HI)
    totals = jnp.sum(lf_all, axis=1, keepdims=True)
    for i in range(pp):
        s = _dot(qbb, k_refs[i][0, 0].astype(jnp.bfloat16))
        scores.append(s + run + suffix[8 * i:8 * (i + 1)])
        run = run + totals[8 * i:8 * (i + 1)]
    run_ref[...] = run
    s_all = jnp.concatenate(scores, axis=1)
    m_prev = m_ref[...]
    m_new = jnp.maximum(m_prev, jnp.max(s_all, axis=1, keepdims=True))
    p = jnp.exp(s_all - m_new)
    alpha = jnp.exp(m_prev - m_new)
    l_ref[...] = alpha * l_ref[...] + jnp.sum(p, axis=1, keepdims=True)
    pv = _dot_nt(p[:, :page].astype(jnp.bfloat16), v_refs[0][0, 0].astype(jnp.bfloat16))
    for i in range(1, pp):
        pv = pv + _dot_nt(p[:, i * page:(i + 1) * page].astype(jnp.bfloat16),
                          v_refs[i][0, 0].astype(jnp.bfloat16))
    acc_ref[...] = alpha * acc_ref[...] + pv
    m_ref[...] = m_new

    @pl.when(j == pl.num_programs(1) - 1)
    def _():
        o = jnp.where(own, acc_ref[...] / l_ref[...], 0.0)
        o_ref[0] = jnp.sum(o, axis=0, keepdims=True)


def _fox_sample(page_table, fq, fk, fv, gt, cache_kt, cache_vt, cache_lf_t, layer, pages_per_step):
    b, _, w = fq.shape
    n_heads = w // HEAD_DIM
    page = cache_kt.shape[-1]
    n_pages = page_table.shape[1]
    pp = pages_per_step

    def page_map(i):
        return lambda bi, j, pt: (layer, pt[bi, n_pages - 1 - (j * pp + i)], 0, 0)

    tok = lambda width: pl.BlockSpec((1, 1, width), lambda bi, j, pt: (bi, 0, 0))
    in_specs = [tok(w), tok(w), tok(w), tok(LANES)]
    in_specs += [pl.BlockSpec((1, 1, w, page), page_map(i)) for i in range(pp)]
    in_specs += [pl.BlockSpec((1, 1, w, page), page_map(i)) for i in range(pp)]
    in_specs += [pl.BlockSpec((1, 1, n_heads, page), page_map(i)) for i in range(pp)]
    grid_spec = pltpu.PrefetchScalarGridSpec(
        num_scalar_prefetch=1, grid=(b, n_pages // pp), in_specs=in_specs, out_specs=tok(w),
        scratch_shapes=[pltpu.VMEM((8, 1), jnp.float32), pltpu.VMEM((8, 1), jnp.float32),
                        pltpu.VMEM((8, 1), jnp.float32), pltpu.VMEM((8, w), jnp.float32)])
    return pl.pallas_call(
        functools.partial(_fox_sample_kernel, pages_per_step=pp, n_heads=n_heads),
        grid_spec=grid_spec,
        out_shape=jax.ShapeDtypeStruct((b, 1, w), jnp.float32),
        compiler_params=_cparams(("arbitrary", "arbitrary")),
    )(page_table, fq, fk, fv, gt, *([cache_kt] * pp), *([cache_vt] * pp), *([cache_lf_t] * pp))


def _pool_kernel(u_ref, hist0_ref, w_ref, scale_ref, o_ref, hist_ref, *, pos0, tl):
    t = pl.program_id(1)

    @pl.when(t == 0)
    def _():
        hist_ref[...] = hist0_ref[0]

    u = u_ref[0]
    ext = jnp.concatenate([hist_ref[...], u], axis=0)
    hist_ref[...] = ext[tl:tl + POOL_HIST]
    sums = []
    acc = ext
    for step in (1, 2, 4, 8):
        acc = acc + pltpu.roll(acc, step, axis=0)
        sums.append(acc[POOL_HIST:])
    width = u.shape[1]
    group = lax.broadcasted_iota(jnp.int32, (tl, width), 1) // (width // len(POOL_WINDOWS))
    wsum = jnp.where(group == 0, sums[0], jnp.where(group == 1, sums[1],
                                                    jnp.where(group == 2, sums[2], sums[3])))
    window = jnp.where(group == 0, POOL_WINDOWS[0],
                       jnp.where(group == 1, POOL_WINDOWS[1],
                                 jnp.where(group == 2, POOL_WINDOWS[2], POOL_WINDOWS[3])))
    pos = pos0 + t * tl + lax.broadcasted_iota(jnp.int32, (tl, width), 0)
    cnt = jnp.minimum(pos + 1, window).astype(jnp.float32)
    d = wsum / cnt - u
    o_ref[0] = _dot(d.astype(jnp.bfloat16), w_ref[...]) * scale_ref[...]


def _pool(u, hist0, w_bd, scale, pos0, tl):
    b, l, w = u.shape
    return pl.pallas_call(
        functools.partial(_pool_kernel, pos0=pos0, tl=tl),
        grid=(b, l // tl),
        in_specs=[pl.BlockSpec((1, tl, w), lambda i, t: (i, t, 0)),
                  pl.BlockSpec((1, POOL_HIST, w), lambda i, t: (i, 0, 0)),
                  _const_spec((w, w)), _const_spec((1, w))],
        out_specs=pl.BlockSpec((1, tl, w), lambda i, t: (i, t, 0)),
        out_shape=jax.ShapeDtypeStruct((b, l, w), jnp.float32),
        scratch_shapes=[pltpu.VMEM((POOL_HIST, w), jnp.float32)],
        compiler_params=_cparams(("arbitrary", "arbitrary")),
    )(u, hist0, w_bd, scale)


def _tail_kernel(x_ref, oa_ref, of_ref, oc_ref, wo_ref, gf_ref, wg_ref, wu_ref, wd_ref, gl_ref, o_ref,
                 *, ff_chunks, final_norm):
    bf = jnp.bfloat16
    mixed = jnp.concatenate([oa_ref[...], of_ref[...], oc_ref[...]], axis=1).astype(bf)
    x = x_ref[...] + _dot(mixed, wo_ref[...])
    h = (x * lax.rsqrt(jnp.mean(x * x, axis=-1, keepdims=True) + RMS_EPS) * gf_ref[...]).astype(bf)
    for c0, c1 in ff_chunks:
        act = _silu(_dot(h, wg_ref[:, c0:c1])) * _dot(h, wu_ref[:, c0:c1])
        x = x + _dot(act.astype(bf), wd_ref[c0:c1, :])
    if final_norm:
        x = x * lax.rsqrt(jnp.mean(x * x, axis=-1, keepdims=True) + RMS_EPS) * gl_ref[...]
    o_ref[...] = x


def _tail(x2d, oa, of, oc, wo, g_ffn, wg, wu, wd, g_last, final_norm, tm):
    n, d = x2d.shape
    d_ff = wg.shape[1]
    step = 1024
    ff_chunks = tuple((c0, min(c0 + step, d_ff)) for c0 in range(0, d_ff, step))
    row_spec = lambda w: pl.BlockSpec((tm, w), lambda i: (i, 0))
    weights = (wo, g_ffn, wg, wu, wd, g_last)
    return pl.pallas_call(
        functools.partial(_tail_kernel, ff_chunks=ff_chunks, final_norm=final_norm),
        grid=(n // tm,),
        in_specs=[row_spec(d), row_spec(oa.shape[1]), row_spec(of.shape[1]), row_spec(oc.shape[1])]
        + [_const_spec(a.shape) for a in weights],
        out_specs=row_spec(d),
        out_shape=jax.ShapeDtypeStruct((n, d), jnp.float32),
        compiler_params=_cparams(("arbitrary",)),
    )(x2d, oa, of, oc, *weights)


def _layer_weights(l, w_gdn, w_fox, w_pool, norm_mix_g, w_in_t, conv_w, a_log, dt_bias, gdn_norm_g,
                   fox_bf, pool_w, pool_scale, w_out, norm_ffn_g, w_gate_up, w_down):
    bf = jnp.bfloat16
    n_hg = w_gdn // HEAD_DIM
    n_hf = w_fox // HEAD_DIM
    d_ff = w_down.shape[1]
    o = 0
    src = {}
    for name, width in (("qkv", 3 * w_gdn), ("z", w_gdn), ("beta", n_hg), ("alpha", n_hg),
                        ("fq", w_fox), ("fk", w_fox), ("fv", w_fox), ("fgate", n_hf), ("pu", w_pool)):
        src[name] = (o, o + width)
        o += width
    wl = w_in_t[:, l, :]
    rows = lambda name: wl[src[name][0]:src[name][1]]
    gate_rows = jnp.concatenate([rows("beta"), rows("alpha"), rows("fgate")], axis=0)
    gate_rows = jnp.pad(gate_rows, ((0, LANES - gate_rows.shape[0]), (0, 0)))
    order = ("qkv", "z", "fq", "fk", "fv", "pu")
    w_t = jnp.concatenate([rows(n) for n in order] + [gate_rows], axis=0).astype(bf)
    seg, o = {}, 0
    for name in order:
        width = src[name][1] - src[name][0]
        seg[name] = (o, o + width)
        o += width
    seg["gates"] = (o, o + LANES)
    gp = jnp.zeros((8, LANES), jnp.float32)
    gp = gp.at[0, GATE_G:GATE_G + n_hg].set(a_log[l])
    gp = gp.at[1, GATE_G:GATE_G + n_hg].set(dt_bias[l])
    gp = gp.at[1, GATE_LOGF:GATE_LOGF + n_hf].set(fox_bf[l])
    n_groups, pg, _ = pool_w[l].shape
    w_bd = jnp.zeros((w_pool, w_pool), jnp.float32)
    for gi in range(n_groups):
        w_bd = w_bd.at[gi * pg:(gi + 1) * pg, gi * pg:(gi + 1) * pg].set(pool_w[l, gi])
    return dict(
        g_mix=norm_mix_g[l][None, :], w_t=w_t, seg=seg, gp=gp, conv_w=conv_w[l],
        gdn_g=gdn_norm_g[l][None, :], w_bd=w_bd.astype(bf), pool_scale=pool_scale[l][None, :],
        wo=w_out[l].astype(bf),
        g_ffn=norm_ffn_g[l][None, :], wg=w_gate_up[l][:, :d_ff].astype(bf),
        wu=w_gate_up[l][:, d_ff:].astype(bf), wd=w_down[l].astype(bf))


def kernel(x_prompt, x_sample, cache_k, cache_v, cache_logf, state_gdn, state_conv, state_pool,
           page_table, norm_mix_g, w_in, conv_w, a_log, dt_bias, gdn_norm_g, fox_bf, pool_w,
           pool_scale, w_out, norm_ffn_g, w_gate_up, w_down, final_norm_g):
    f32 = jnp.float32
    bp, seq, d = x_prompt.shape
    bs, dec_seq, _ = x_sample.shape
    depth, n_pool, page, n_hf, _ = cache_k.shape
    n_hg = state_gdn.shape[2]
    w_gdn = n_hg * HEAD_DIM
    w_fox = n_hf * HEAD_DIM
    w_pool = state_pool.shape[-1]
    pool_buf = state_pool.shape[2]
    conv_hist = state_conv.shape[2]
    past_len = page_table.shape[1] * page
    C = GDN_CHUNK
    assert dec_seq == 1 and seq % LANES == 0 and conv_hist == 3 and pool_buf == POOL_HIST - 1
    assert bs <= LANES

    tm_p = min(512, seq)
    tq = min(512, seq)
    tl = min(512, seq)
    pps = max(p for p in (16, 8, 4, 2, 1) if page_table.shape[1] % p == 0)
    g_last = final_norm_g[None, :]

    xp = x_prompt.reshape(bp * seq, d)
    xs = x_sample.reshape(bs, d)
    cache_kt = jnp.transpose(cache_k, (0, 1, 3, 4, 2)).reshape(depth, n_pool, w_fox, page)
    cache_vt = jnp.transpose(cache_v, (0, 1, 3, 4, 2)).reshape(depth, n_pool, w_fox, page)
    cache_lf_t = jnp.swapaxes(cache_logf, 2, 3)
    w_in_t = jnp.transpose(w_in, (2, 0, 1))

    st_p, st_s = [], []
    for l in range(depth):
        lw = _layer_weights(l, w_gdn, w_fox, w_pool, norm_mix_g, w_in_t, conv_w, a_log, dt_bias,
                            gdn_norm_g, fox_bf, pool_w, pool_scale, w_out, norm_ffn_g, w_gate_up, w_down)
        final = l == depth - 1

        qkv, z, fq, fkt, fvt, pu, gt, cum, lft, cumt = _in_proj(
            xp.reshape(bp, seq, d), lw["g_mix"], lw["w_t"], lw["gp"], lw["seg"], tm_p)
        o_a, s_fin = _gdn(qkv, z, gt, lw["conv_w"], jnp.zeros((bp, 8, 3 * w_gdn), f32),
                          jnp.zeros((bp, n_hg, HEAD_DIM, HEAD_DIM), f32), lw["gdn_g"], C)
        o_f = _fox_prompt(fq, fkt, fvt, cum, cumt, tq)
        o_c = _pool(pu, jnp.zeros((bp, POOL_HIST, w_pool), f32), lw["w_bd"], lw["pool_scale"], 0, tl)
        xp = _tail(xp, o_a.reshape(bp * seq, w_gdn), o_f.reshape(bp * seq, w_fox),
                   o_c.reshape(bp * seq, w_pool), lw["wo"], lw["g_ffn"],
                   lw["wg"], lw["wu"], lw["wd"], g_last, final, tm_p if tm_p <= 256 else 256)
        heads_last = lambda a: jnp.transpose(a.reshape(bp, n_hf, HEAD_DIM, seq), (0, 3, 1, 2))
        st_p.append((heads_last(fkt), heads_last(fvt), jnp.swapaxes(lft[:, :n_hf, :], 1, 2), s_fin,
                     qkv[:, seq - conv_hist:], pu[:, seq - pool_buf:]))

        xs_rows = jnp.pad(xs, ((0, LANES - bs), (0, 0)))[None]
        qkv, z, fq, fkt, fvt, pu, gt, _, _, _ = _in_proj(xs_rows, lw["g_mix"], lw["w_t"], lw["gp"],
                                                         lw["seg"], LANES)
        qkv, z, fq, pu, gt = (a[0, :bs] for a in (qkv, z, fq, pu, gt))
        fk, fv = (jnp.swapaxes(a[0, :, :bs], 0, 1) for a in (fkt, fvt))
        pad_c = lambda a: jnp.pad(a[:, None, :], ((0, 0), (0, C - 1), (0, 0)))
        conv0 = jnp.pad(state_conv[l], ((0, 0), (8 - conv_hist, 0), (0, 0)))
        o_a, s_fin = _gdn(pad_c(qkv), pad_c(z), pad_c(gt), lw["conv_w"], conv0, state_gdn[l],
                          lw["gdn_g"], 1)
        o_f = _fox_sample(page_table, fq[:, None, :], fk[:, None, :], fv[:, None, :], gt[:, None, :],
                          cache_kt, cache_vt, cache_lf_t, l, pps)
        hist0 = jnp.pad(state_pool[l], ((0, 0), (POOL_HIST - pool_buf, 0), (0, 0)))
        o_c = _pool(jnp.pad(pu[:, None, :], ((0, 0), (0, 7), (0, 0))), hist0, lw["w_bd"],
                    lw["pool_scale"], past_len, 8)
        xs = _tail(xs, o_a[:, 0], o_f[:, 0], o_c[:, 0], lw["wo"], lw["g_ffn"],
                   lw["wg"], lw["wu"], lw["wd"], g_last, final, bs)
        st_s.append((fk.reshape(bs, 1, n_hf, HEAD_DIM), fv.reshape(bs, 1, n_hf, HEAD_DIM),
                     gt[:, None, GATE_LOGF:GATE_LOGF + n_hf], s_fin,
                     jnp.concatenate([state_conv[l][:, 1:], qkv[:, None, :]], axis=1),
                     jnp.concatenate([state_pool[l][:, 1:], pu[:, None, :]], axis=1)))

    outs = [xp.reshape(bp, seq, d), xs.reshape(bs, 1, d)]
    for st in (st_p, st_s):
        for i in range(6):
            outs.append(jnp.stack([s[i] for s in st]))
    return tuple(outs)
```

```python
import functools

import jax
import jax.numpy as jnp
from jax import lax
from jax.experimental import pallas as pl
from jax.experimental.pallas import tpu as pltpu

HEAD_DIM = 64
POOL_WINDOWS = (2, 4, 8, 16)
POOL_HIST = 16
GDN_CHUNK = 64
SOLVE_BLOCK = 8
RMS_EPS = 1e-6
L2_EPS = 1e-6
LANES = 128
VMEM_LIMIT = 56 * 1024 * 1024
HI = lax.Precision.HIGHEST

GATE_BETA = 0
GATE_G = 6
GATE_LOGF = 12


def _cparams(sem):
    return pltpu.CompilerParams(dimension_semantics=sem, vmem_limit_bytes=VMEM_LIMIT)


def _const_spec(shape):
    nd = len(shape)
    return pl.BlockSpec(shape, lambda *_: (0,) * nd, pipeline_mode=pl.Buffered(1))


def _softplus(x):
    return jnp.maximum(x, 0.0) + jnp.log1p(jnp.exp(-jnp.abs(x)))


def _silu(x):
    return x * jax.nn.sigmoid(x)


def _dot_nt(a, b, precision=None):
    return lax.dot_general(a, b, (((1,), (1,)), ((), ())), precision=precision,
                           preferred_element_type=jnp.float32)


def _dot(a, b, precision=None):
    return jnp.dot(a, b, precision=precision, preferred_element_type=jnp.float32)


def _in_proj_kernel(x_ref, g_ref, wt_ref, gp_ref, qkv_ref, z_ref, fq_ref, fkt_ref, fvt_ref, pu_ref,
                    gt_ref, cum_ref, lft_ref, cumt_ref, carry_ref, *, seg, tm):
    x = x_ref[0]
    h = x * lax.rsqrt(jnp.mean(x * x, axis=-1, keepdims=True) + RMS_EPS) * g_ref[...]
    hb = h.astype(jnp.bfloat16)
    for name, o_ref in (("qkv", qkv_ref), ("z", z_ref), ("fq", fq_ref), ("pu", pu_ref)):
        r0, r1 = seg[name]
        o_ref[0] = _dot_nt(hb, wt_ref[r0:r1, :])
    for name, o_ref in (("fk", fkt_ref), ("fv", fvt_ref)):
        r0, r1 = seg[name]
        o_ref[0] = _dot_nt(wt_ref[r0:r1, :], hb)
    r0, r1 = seg["gates"]
    raw = _dot_nt(hb, wt_ref[r0:r1, :])
    lane = lax.broadcasted_iota(jnp.int32, raw.shape, 1)
    a_log = gp_ref[0:1, :]
    shifted = raw + gp_ref[1:2, :]
    beta = jax.nn.sigmoid(raw)
    g = -jnp.exp(a_log) * _softplus(shifted)
    logf = -_softplus(-shifted)
    gt = jnp.where(lane < GATE_G, beta,
                   jnp.where(lane < GATE_LOGF, g,
                             jnp.where(lane < GATE_LOGF + 6, logf, 0.0)))
    gt_ref[0] = gt

    @pl.when(pl.program_id(1) == 0)
    def _():
        carry_ref[...] = jnp.zeros_like(carry_ref)

    r = lax.broadcasted_iota(jnp.int32, (LANES, LANES), 0)
    c = lax.broadcasted_iota(jnp.int32, (LANES, LANES), 1)
    tril = (c <= r).astype(jnp.float32)
    carry = carry_ref[...]
    for s in range(tm // LANES):
        rows = slice(s * LANES, (s + 1) * LANES)
        cs = _dot(tril, gt[rows], HI) + carry
        cum_ref[0, rows, :] = cs
        cumt_ref[0, :, rows] = cs.T[GATE_LOGF:GATE_LOGF + 8, :]
        lft_ref[0, :, rows] = gt[rows].T[GATE_LOGF:GATE_LOGF + 8, :]
        carry = cs[LANES - 1:LANES, :]
    carry_ref[...] = carry


def _in_proj(x, g_mix, w_t, gate_params, seg, tm):
    b, l, d = x.shape
    row = lambda w: (jax.ShapeDtypeStruct((b, l, w), jnp.float32),
                     pl.BlockSpec((1, tm, w), lambda i, t: (i, t, 0)))
    col = lambda w: (jax.ShapeDtypeStruct((b, w, l), jnp.float32),
                     pl.BlockSpec((1, w, tm), lambda i, t: (i, 0, t)))
    width = lambda name: seg[name][1] - seg[name][0]
    outs = [row(width("qkv")), row(width("z")), row(width("fq")), col(width("fk")), col(width("fv")),
            row(width("pu")), row(LANES), row(LANES), col(8), col(8)]
    return pl.pallas_call(
        functools.partial(_in_proj_kernel, seg=seg, tm=tm),
        grid=(b, l // tm),
        in_specs=[pl.BlockSpec((1, tm, d), lambda i, t: (i, t, 0)), _const_spec((1, d)),
                  _const_spec(w_t.shape), _const_spec((8, LANES))],
        out_specs=[o[1] for o in outs],
        out_shape=[o[0] for o in outs],
        scratch_shapes=[pltpu.VMEM((1, LANES), jnp.float32)],
        compiler_params=_cparams(("arbitrary", "arbitrary")),
    )(x, g_mix, w_t, gate_params)


def _forward_substitute(a_blocks, x_blocks, lo, hi):
    for j in range(lo, hi - 1):
        for a_sys, x_sys in zip(a_blocks, x_blocks):
            row = jnp.broadcast_to(x_sys[j // 8][j % 8:j % 8 + 1, :], x_sys[0].shape)
            for rr in range((j + 1) // 8, hi // 8):
                x_sys[rr] = x_sys[rr] - a_sys[rr][:, j:j + 1] * row


def _solve_unit_lower(a_mats, rhs, n):
    bs = SOLVE_BLOCK
    a_blocks = [[a[r * 8:(r + 1) * 8, :] for r in range(n // 8)] for a in a_mats]
    x_blocks = [[x[r * 8:(r + 1) * 8, :] for r in range(n // 8)] for x in rhs]
    for lo in range(0, n, bs):
        if lo > 0:
            for a, x_sys in zip(a_mats, x_blocks):
                done = jnp.concatenate(x_sys[:lo // 8], axis=0)
                cur = jnp.concatenate(x_sys[lo // 8:(lo + bs) // 8], axis=0)
                cur = cur - _dot(a[lo:lo + bs, :lo], done, HI)
                x_sys[lo // 8:(lo + bs) // 8] = [cur[r * 8:(r + 1) * 8, :] for r in range(bs // 8)]
        _forward_substitute(a_blocks, x_blocks, lo, lo + bs)
    return [jnp.concatenate(x_sys, axis=0) for x_sys in x_blocks]


def _gdn_prep_kernel(qkv_ref, gt_ref, cw_ref, conv0_ref, mats_ref, qd_ref, glast_ref, hist_ref,
                     *, n_heads, valid, chunks):
    c = pl.program_id(1)
    C = GDN_CHUNK
    D = HEAD_DIM
    W = n_heads * D

    @pl.when(c == 0)
    def _():
        hist_ref[...] = conv0_ref[0]

    x = qkv_ref[0]
    rows = chunks * C
    ext = jnp.concatenate([hist_ref[...], x], axis=0)
    conv_all = ext[5:5 + rows] * cw_ref[0:1, :]
    for j in range(1, 4):
        conv_all = conv_all + ext[5 + j:5 + j + rows] * cw_ref[j:j + 1, :]
    conv_all = _silu(conv_all)
    hist_ref[...] = x[rows - 8:rows]

    ri = lax.broadcasted_iota(jnp.int32, (C, C), 0)
    ci = lax.broadcasted_iota(jnp.int32, (C, C), 1)
    tri = ci <= ri
    strict = ci < ri

    pending = []
    for n in range(chunks):
        conv = conv_all[n * C:(n + 1) * C]
        gt = gt_ref[0, n * C:(n + 1) * C, :]
        if valid < C:
            live = lax.broadcasted_iota(jnp.int32, (C, 1), 0) < valid
            conv = jnp.where(live, conv, 0.0)
            gt = jnp.where(live, gt, 0.0)
        g_cum = _dot(tri.astype(jnp.float32), gt, HI)
        g_cum_t = g_cum.T
        glast_ref[0, n] = jnp.broadcast_to(g_cum[C - 1:C, :], (8, LANES))
        a_mats, rhs_all, side = [], [], []
        for h in range(n_heads):
            q = conv[:, h * D:(h + 1) * D]
            k = conv[:, W + h * D:W + (h + 1) * D]
            v = conv[:, 2 * W + h * D:2 * W + (h + 1) * D]
            q = q * lax.rsqrt(jnp.sum(q * q, axis=-1, keepdims=True) + L2_EPS) * (D ** -0.5)
            k = k * lax.rsqrt(jnp.sum(k * k, axis=-1, keepdims=True) + L2_EPS)
            beta = gt[:, GATE_BETA + h:GATE_BETA + h + 1]
            gc = g_cum[:, GATE_G + h:GATE_G + h + 1]
            gr = g_cum_t[GATE_G + h:GATE_G + h + 1, :]
            g_last = gc[C - 1:C, :]
            decay = jnp.where(tri, jnp.exp(jnp.where(tri, gc - gr, 0.0)), 0.0)
            e_g = jnp.exp(gc)
            kb = k * beta
            a_mats.append(jnp.where(strict, _dot_nt(kb, k) * decay, 0.0))
            rhs_all.append(jnp.concatenate([v * beta, kb * e_g], axis=1))
            side.append((_dot_nt(q, k) * decay, q * e_g, (k * jnp.exp(g_last - gc)).T))
        pending.append((a_mats, rhs_all, side))

    sols = _solve_unit_lower([a for p in pending for a in p[0]], [r for p in pending for r in p[1]], C)
    for n, (_, _, side) in enumerate(pending):
        for h in range(n_heads):
            sol = sols[n * n_heads + h]
            qk, qd, kd_t = side[h]
            mats_ref[0, n, h, 0:C, :] = _dot(qk, sol)
            mats_ref[0, n, h, C:2 * C, :] = _dot(kd_t, sol)
            qd_ref[0, n, h] = qd


def _gdn_scan_kernel(mats_ref, qd_ref, glast_ref, z_ref, s0_ref, gn_ref, o_ref, s_ref,
                     *, n_heads, chunks):
    C = GDN_CHUNK
    D = HEAD_DIM

    @pl.when(pl.program_id(1) == 0)
    def _():
        s_ref[...] = s0_ref[...]

    for n in range(chunks):
        outs = []
        for h in range(n_heads):
            g_last = glast_ref[0, n][0:1, GATE_G + h:GATE_G + h + 1]
            s_prev = s_ref[0, h]
            s_low = jnp.concatenate([jnp.zeros((D, D), jnp.float32), s_prev], axis=0)
            moved = _dot(mats_ref[0, n, h], s_low)
            o = _dot(qd_ref[0, n, h], s_prev) - moved[:C] + mats_ref[0, n, h, 0:C, 0:D]
            s_ref[0, h] = s_prev * jnp.exp(g_last) - moved[C:] + mats_ref[0, n, h, C:2 * C, 0:D]
            o = o * lax.rsqrt(jnp.mean(o * o, axis=-1, keepdims=True) + RMS_EPS) * gn_ref[...]
            outs.append(o * _silu(z_ref[0, n * C:(n + 1) * C, h * D:(h + 1) * D]))
        o_ref[0, n * C:(n + 1) * C, :] = jnp.concatenate(outs, axis=1)


def _gdn(qkv, z, gt, conv_w, conv0, s0, gdn_g, valid):
    b, l, w3 = qkv.shape
    w = w3 // 3
    n_heads = w // HEAD_DIM
    C = GDN_CHUNK
    D = HEAD_DIM
    nc = l // C
    f32 = jnp.float32
    per_b = lambda shape: pl.BlockSpec((1,) + shape, lambda i, j: (i,) + (0,) * len(shape))
    chunk = lambda n, shape: pl.BlockSpec((1, n) + shape, lambda i, j: (i, j) + (0,) * len(shape))
    tok = lambda n, width: pl.BlockSpec((1, n * C, width), lambda i, j: (i, j, 0))
    cpp = max(n for n in (2, 1) if nc % n == 0)
    mats, qd, glast = pl.pallas_call(
        functools.partial(_gdn_prep_kernel, n_heads=n_heads, valid=valid, chunks=cpp),
        grid=(b, nc // cpp),
        in_specs=[tok(cpp, w3), tok(cpp, LANES), _const_spec((4, w3)), per_b((8, w3))],
        out_specs=[chunk(cpp, (n_heads, 2 * C, 2 * D)), chunk(cpp, (n_heads, C, D)), chunk(cpp, (8, LANES))],
        out_shape=[jax.ShapeDtypeStruct((b, nc, n_heads, 2 * C, 2 * D), f32),
                   jax.ShapeDtypeStruct((b, nc, n_heads, C, D), f32),
                   jax.ShapeDtypeStruct((b, nc, 8, LANES), f32)],
        scratch_shapes=[pltpu.VMEM((8, w3), f32)],
        compiler_params=_cparams(("arbitrary", "arbitrary")),
    )(qkv, gt, conv_w, conv0)
    cps = max(n for n in (4, 2, 1) if nc % n == 0)
    return pl.pallas_call(
        functools.partial(_gdn_scan_kernel, n_heads=n_heads, chunks=cps),
        grid=(b, nc // cps),
        in_specs=[chunk(cps, (n_heads, 2 * C, 2 * D)), chunk(cps, (n_heads, C, D)), chunk(cps, (8, LANES)),
                  tok(cps, w), per_b((n_heads, D, D)), _const_spec((1, D))],
        out_specs=[tok(cps, w), per_b((n_heads, D, D))],
        out_shape=[jax.ShapeDtypeStruct((b, l, w), f32), jax.ShapeDtypeStruct((b, n_heads, D, D), f32)],
        compiler_params=_cparams(("arbitrary", "arbitrary")),
    )(mats, qd, glast, z, s0, gdn_g)


def _split3(x):
    hi = x.astype(jnp.bfloat16).astype(jnp.float32)
    mid = (x - hi).astype(jnp.bfloat16).astype(jnp.float32)
    return hi, mid, (x - hi) - mid


def _fox_prompt_kernel(q_ref, kt_ref, vt_ref, cum_ref, cum_t_ref, o_ref, kta_ref, vtb_ref, m_ref, l_ref,
                       acc_ref, *, tq):
    hp = pl.program_id(1)
    qi = pl.program_id(2)
    bf = jnp.bfloat16
    f32 = jnp.float32
    seq = kt_ref.shape[2]
    log2e = 1.4426950408889634
    lane = lax.broadcasted_iota(jnp.int32, (tq, LANES), 1)
    low = lane < HEAD_DIM

    @pl.when(qi == 0)
    def _():
        row = lax.broadcasted_iota(jnp.int32, (LANES, tq), 0)
        sub = lax.broadcasted_iota(jnp.int32, (8, tq), 0)
        for t in range(seq // tq):
            cols = slice(t * tq, (t + 1) * tq)
            kt = kt_ref[0, :, cols]
            vtb_ref[:, cols] = vt_ref[0, :, cols].astype(bf)
            c_cols = cum_t_ref[0, :, cols]
            for hh in range(2):
                c_col = jnp.sum(jnp.where(sub == 2 * hp + hh, c_cols, 0.0), axis=0, keepdims=True)
                hi, mid, lo = _split3(-log2e * c_col)
                r = row - (HEAD_DIM if hh == 0 else 0)
                own = (row < HEAD_DIM) == (hh == 0)
                aug = jnp.where(r == 3, hi, jnp.where(r == 4, mid, jnp.where(r == 5, lo, 0.0)))
                aug = jnp.where((r >= 0) & (r < 3), 1.0, aug)
                kta_ref[hh, :, cols] = jnp.where(own, kt, aug).astype(bf)

    q = q_ref[0] * (HEAD_DIM ** -0.5 * log2e)
    cum = cum_ref[0]
    head_lane = lax.broadcasted_iota(jnp.int32, cum.shape, 1) - GATE_LOGF - 2 * hp
    q_heads = []
    for hh in range(2):
        c_row = jnp.sum(jnp.where(head_lane == hh, cum, 0.0), axis=1, keepdims=True)
        hi, mid, lo = _split3(log2e * c_row)
        r = lane - (HEAD_DIM if hh == 0 else 0)
        own = low == (hh == 0)
        aug = jnp.where(r == 0, hi, jnp.where(r == 1, mid, jnp.where(r == 2, lo, 0.0)))
        aug = jnp.where((r >= 3) & (r < 6), 1.0, aug)
        q_heads.append(jnp.where(own, q, aug).astype(bf))

    m_ref[...] = jnp.full(m_ref.shape, -jnp.inf, f32)
    l_ref[...] = jnp.zeros(l_ref.shape, f32)
    acc_ref[...] = jnp.zeros(acc_ref.shape, f32)

    def tile(ki, masked):
        ks = pl.multiple_of(ki * tq, tq)
        vt = vtb_ref[:, pl.ds(ks, tq)]
        for hh in range(2):
            s = _dot(q_heads[hh], kta_ref[hh, :, pl.ds(ks, tq)])
            if masked:
                r = lax.broadcasted_iota(jnp.int32, s.shape, 0)
                cc = lax.broadcasted_iota(jnp.int32, s.shape, 1)
                s = jnp.where(cc <= r, s, -jnp.inf)
            m_prev = m_ref[hh]
            m_new = jnp.maximum(m_prev, jnp.max(s, axis=1, keepdims=True))
            p = jnp.exp2(s - jnp.tile(m_new, (1, tq // LANES)))
            alpha = jnp.exp2(m_prev - m_new)
            l_ref[hh] = alpha * l_ref[hh] + jnp.sum(p, axis=1, keepdims=True)
            acc_ref[hh] = alpha * acc_ref[hh] + _dot_nt(p.astype(bf), vt)
            m_ref[hh] = m_new

    def body(ki, carry):
        tile(ki, False)
        return carry

    lax.fori_loop(0, qi, body, 0)
    tile(qi, True)
    o_ref[0] = jnp.where(low, acc_ref[0] / l_ref[0], acc_ref[1] / l_ref[1])


def _fox_prompt(fq, fkt, fvt, cum, cum_t, tq):
    b, l, w = fq.shape
    n_pairs = w // LANES
    q_spec = pl.BlockSpec((1, tq, LANES), lambda i, hp, qi: (i, qi, hp))
    kv_spec = pl.BlockSpec((1, LANES, l), lambda i, hp, qi: (i, hp, 0))
    return pl.pallas_call(
        functools.partial(_fox_prompt_kernel, tq=tq),
        grid=(b, n_pairs, l // tq),
        in_specs=[q_spec, kv_spec, kv_spec,
                  pl.BlockSpec((1, tq, LANES), lambda i, hp, qi: (i, qi, 0)),
                  pl.BlockSpec((1, 8, l), lambda i, hp, qi: (i, 0, 0))],
        out_specs=q_spec,
        out_shape=jax.ShapeDtypeStruct((b, l, w), jnp.float32),
        scratch_shapes=[pltpu.VMEM((2, LANES, l), jnp.bfloat16), pltpu.VMEM((LANES, l), jnp.bfloat16),
                        pltpu.VMEM((2, tq, LANES), jnp.float32), pltpu.VMEM((2, tq, LANES), jnp.float32),
                        pltpu.VMEM((2, tq, LANES), jnp.float32)],
        compiler_params=_cparams(("arbitrary", "arbitrary", "arbitrary")),
    )(fq, fkt, fvt, cum, cum_t)


def _fox_sample_kernel(pt_ref, q_ref, kn_ref, vn_ref, gt_ref, *refs, pages_per_step, n_heads):
    del pt_ref
    pp = pages_per_step
    k_refs, v_refs, lf_refs = refs[:pp], refs[pp:2 * pp], refs[2 * pp:3 * pp]
    o_ref, m_ref, l_ref, run_ref, acc_ref = refs[3 * pp:]
    j = pl.program_id(1)
    w = n_heads * HEAD_DIM
    page = k_refs[0].shape[-1]

    sub = lax.broadcasted_iota(jnp.int32, (8, w), 0)
    own = lax.broadcasted_iota(jnp.int32, (8, w), 1) // HEAD_DIM == sub
    qb = jnp.where(own, q_ref[0] * (HEAD_DIM ** -0.5), 0.0)

    @pl.when(j == 0)
    def _():
        m_ref[...] = jnp.sum(qb * kn_ref[0], axis=1, keepdims=True)
        l_ref[...] = jnp.ones_like(l_ref)
        acc_ref[...] = jnp.broadcast_to(vn_ref[0], acc_ref.shape)
        gt = gt_ref[0]
        pick = (lax.broadcasted_iota(jnp.int32, (8, LANES), 1) - GATE_LOGF
                == lax.broadcasted_iota(jnp.int32, (8, LANES), 0))
        run_ref[...] = jnp.sum(jnp.where(pick, gt, 0.0), axis=1, keepdims=True)

    r = lax.broadcasted_iota(jnp.int32, (page, page), 0)
    c = lax.broadcasted_iota(jnp.int32, (page, page), 1)
    later = (r > c).astype(jnp.float32)
    qbb = qb.astype(jnp.bfloat16)
    run = run_ref[...]
    scores = []
    pad = jnp.zeros((8 - n_heads, page), jnp.float32)
    lf_all = jnp.concatenate([x for i in range(pp) for x in (lf_refs[i][0, 0], pad)], axis=0)
    suffix = _dot(lf_all, later, HI)
    totals = jnp.sum(lf_all, axis=1, keepdims=True)
    for i in range(pp):
        s = _dot(qbb, k_refs[i][0, 0].astype(jnp.bfloat16))
        scores.append(s + run + suffix[8 * i:8 * (i + 1)])
        run = run + totals[8 * i:8 * (i + 1)]
    run_ref[...] = run
    s_all = jnp.concatenate(scores, axis=1)
    m_prev = m_ref[...]
    m_new = jnp.maximum(m_prev, jnp.max(s_all, axis=1, keepdims=True))
    p = jnp.exp(s_all - m_new)
    alpha = jnp.exp(m_prev - m_new)
    l_ref[...] = alpha * l_ref[...] + jnp.sum(p, axis=1, keepdims=True)
    pv = _dot_nt(p[:, :page].astype(jnp.bfloat16), v_refs[0][0, 0].astype(jnp.bfloat16))
    for i in range(1, pp):
        pv = pv + _dot_nt(p[:, i * page:(i + 1) * page].astype(jnp.bfloat16),
                          v_refs[i][0, 0].astype(jnp.bfloat16))
    acc_ref[...] = alpha * acc_ref[...] + pv
    m_ref[...] = m_new

    @pl.when(j == pl.num_programs(1) - 1)
    def _():
        o = jnp.where(own, acc_ref[...] / l_ref[...], 0.0)
        o_ref[0] = jnp.sum(o, axis=0, keepdims=True)


def _fox_sample(page_table, fq, fk, fv, gt, cache_kt, cache_vt, cache_lf_t, layer, pages_per_step):
    b, _, w = fq.shape
    n_heads = w // HEAD_DIM
    page = cache_kt.shape[-1]
    n_pages = page_table.shape[1]
    pp = pages_per_step

    def page_map(i):
        return lambda bi, j, pt: (layer, pt[bi, n_pages - 1 - (j * pp + i)], 0, 0)

    tok = lambda width: pl.BlockSpec((1, 1, width), lambda bi, j, pt: (bi, 0, 0))
    in_specs = [tok(w), tok(w), tok(w), tok(LANES)]
    in_specs += [pl.BlockSpec((1, 1, w, page), page_map(i)) for i in range(pp)]
    in_specs += [pl.BlockSpec((1, 1, w, page), page_map(i)) for i in range(pp)]
    in_specs += [pl.BlockSpec((1, 1, n_heads, page), page_map(i)) for i in range(pp)]
    grid_spec = pltpu.PrefetchScalarGridSpec(
        num_scalar_prefetch=1, grid=(b, n_pages // pp), in_specs=in_specs, out_specs=tok(w),
        scratch_shapes=[pltpu.VMEM((8, 1), jnp.float32), pltpu.VMEM((8, 1), jnp.float32),
                        pltpu.VMEM((8, 1), jnp.float32), pltpu.VMEM((8, w), jnp.float32)])
    return pl.pallas_call(
        functools.partial(_fox_sample_kernel, pages_per_step=pp, n_heads=n_heads),
        grid_spec=grid_spec,
        out_shape=jax.ShapeDtypeStruct((b, 1, w), jnp.float32),
        compiler_params=_cparams(("arbitrary", "arbitrary")),
    )(page_table, fq, fk, fv, gt, *([cache_kt] * pp), *([cache_vt] * pp), *([cache_lf_t] * pp))


def _pool_kernel(u_ref, hist0_ref, w_ref, scale_ref, o_ref, hist_ref, *, pos0, tl):
    t = pl.program_id(1)

    @pl.when(t == 0)
    def _():
        hist_ref[...] = hist0_ref[0]

    u = u_ref[0]
    ext = jnp.concatenate([hist_ref[...], u], axis=0)
    hist_ref[...] = ext[tl:tl + POOL_HIST]
    sums = []
    acc = ext
    for step in (1, 2, 4, 8):
        acc = acc + pltpu.roll(acc, step, axis=0)
        sums.append(acc[POOL_HIST:])
    width = u.shape[1]
    group = lax.broadcasted_iota(jnp.int32, (tl, width), 1) // (width // len(POOL_WINDOWS))
    wsum = jnp.where(group == 0, sums[0], jnp.where(group == 1, sums[1],
                                                    jnp.where(group == 2, sums[2], sums[3])))
    window = jnp.where(group == 0, POOL_WINDOWS[0],
                       jnp.where(group == 1, POOL_WINDOWS[1],
                                 jnp.where(group == 2, POOL_WINDOWS[2], POOL_WINDOWS[3])))
    pos = pos0 + t * tl + lax.broadcasted_iota(jnp.int32, (tl, width), 0)
    cnt = jnp.minimum(pos + 1, window).astype(jnp.float32)
    d = wsum / cnt - u
    o_ref[0] = _dot(d.astype(jnp.bfloat16), w_ref[...]) * scale_ref[...]


def _pool(u, hist0, w_bd, scale, pos0, tl):
    b, l, w = u.shape
    return pl.pallas_call(
        functools.partial(_pool_kernel, pos0=pos0, tl=tl),
        grid=(b, l // tl),
        in_specs=[pl.BlockSpec((1, tl, w), lambda i, t: (i, t, 0)),
                  pl.BlockSpec((1, POOL_HIST, w), lambda i, t: (i, 0, 0)),
                  _const_spec((w, w)), _const_spec((1, w))],
        out_specs=pl.BlockSpec((1, tl, w), lambda i, t: (i, t, 0)),
        out_shape=jax.ShapeDtypeStruct((b, l, w), jnp.float32),
        scratch_shapes=[pltpu.VMEM((POOL_HIST, w), jnp.float32)],
        compiler_params=_cparams(("arbitrary", "arbitrary")),
    )(u, hist0, w_bd, scale)


def _tail_kernel(x_ref, oa_ref, of_ref, oc_ref, wo_ref, gf_ref, wg_ref, wu_ref, wd_ref, gl_ref, o_ref,
                 *, ff_chunks, final_norm):
    bf = jnp.bfloat16
    mixed = jnp.concatenate([oa_ref[...], of_ref[...], oc_ref[...]], axis=1).astype(bf)
    x = x_ref[...] + _dot(mixed, wo_ref[...])
    h = (x * lax.rsqrt(jnp.mean(x * x, axis=-1, keepdims=True) + RMS_EPS) * gf_ref[...]).astype(bf)
    for c0, c1 in ff_chunks:
        act = _silu(_dot(h, wg_ref[:, c0:c1])) * _dot(h, wu_ref[:, c0:c1])
        x = x + _dot(act.astype(bf), wd_ref[c0:c1, :])
    if final_norm:
        x = x * lax.rsqrt(jnp.mean(x * x, axis=-1, keepdims=True) + RMS_EPS) * gl_ref[...]
    o_ref[...] = x


def _tail(x2d, oa, of, oc, wo, g_ffn, wg, wu, wd, g_last, final_norm, tm):
    n, d = x2d.shape
    d_ff = wg.shape[1]
    step = 1024
    ff_chunks = tuple((c0, min(c0 + step, d_ff)) for c0 in range(0, d_ff, step))
    row_spec = lambda w: pl.BlockSpec((tm, w), lambda i: (i, 0))
    weights = (wo, g_ffn, wg, wu, wd, g_last)
    return pl.pallas_call(
        functools.partial(_tail_kernel, ff_chunks=ff_chunks, final_norm=final_norm),
        grid=(n // tm,),
        in_specs=[row_spec(d), row_spec(oa.shape[1]), row_spec(of.shape[1]), row_spec(oc.shape[1])]
        + [_const_spec(a.shape) for a in weights],
        out_specs=row_spec(d),
        out_shape=jax.ShapeDtypeStruct((n, d), jnp.float32),
        compiler_params=_cparams(("arbitrary",)),
    )(x2d, oa, of, oc, *weights)


def _layer_weights(l, w_gdn, w_fox, w_pool, norm_mix_g, w_in_t, conv_w, a_log, dt_bias, gdn_norm_g,
                   fox_bf, pool_w, pool_scale, w_out, norm_ffn_g, w_gate_up, w_down):
    bf = jnp.bfloat16
    n_hg = w_gdn // HEAD_DIM
    n_hf = w_fox // HEAD_DIM
    d_ff = w_down.shape[1]
    o = 0
    src = {}
    for name, width in (("qkv", 3 * w_gdn), ("z", w_gdn), ("beta", n_hg), ("alpha", n_hg),
                        ("fq", w_fox), ("fk", w_fox), ("fv", w_fox), ("fgate", n_hf), ("pu", w_pool)):
        src[name] = (o, o + width)
        o += width
    wl = w_in_t[:, l, :]
    rows = lambda name: wl[src[name][0]:src[name][1]]
    gate_rows = jnp.concatenate([rows("beta"), rows("alpha"), rows("fgate")], axis=0)
    gate_rows = jnp.pad(gate_rows, ((0, LANES - gate_rows.shape[0]), (0, 0)))
    order = ("qkv", "z", "fq", "fk", "fv", "pu")
    w_t = jnp.concatenate([rows(n) for n in order] + [gate_rows], axis=0).astype(bf)
    seg, o = {}, 0
    for name in order:
        width = src[name][1] - src[name][0]
        seg[name] = (o, o + width)
        o += width
    seg["gates"] = (o, o + LANES)
    gp = jnp.zeros((8, LANES), jnp.float32)
    gp = gp.at[0, GATE_G:GATE_G + n_hg].set(a_log[l])
    gp = gp.at[1, GATE_G:GATE_G + n_hg].set(dt_bias[l])
    gp = gp.at[1, GATE_LOGF:GATE_LOGF + n_hf].set(fox_bf[l])
    n_groups, pg, _ = pool_w[l].shape
    w_bd = jnp.zeros((w_pool, w_pool), jnp.float32)
    for gi in range(n_groups):
        w_bd = w_bd.at[gi * pg:(gi + 1) * pg, gi * pg:(gi + 1) * pg].set(pool_w[l, gi])
    return dict(
        g_mix=norm_mix_g[l][None, :], w_t=w_t, seg=seg, gp=gp, conv_w=conv_w[l],
        gdn_g=gdn_norm_g[l][None, :], w_bd=w_bd.astype(bf), pool_scale=pool_scale[l][None, :],
        wo=w_out[l].astype(bf),
        g_ffn=norm_ffn_g[l][None, :], wg=w_gate_up[l][:, :d_ff].astype(bf),
        wu=w_gate_up[l][:, d_ff:].astype(bf), wd=w_down[l].astype(bf))


def kernel(x_prompt, x_sample, cache_k, cache_v, cache_logf, state_gdn, state_conv, state_pool,
           page_table, norm_mix_g, w_in, conv_w, a_log, dt_bias, gdn_norm_g, fox_bf, pool_w,
           pool_scale, w_out, norm_ffn_g, w_gate_up, w_down, final_norm_g):
    f32 = jnp.float32
    bp, seq, d = x_prompt.shape
    bs, dec_seq, _ = x_sample.shape
    depth, n_pool, page, n_hf, _ = cache_k.shape
    n_hg = state_gdn.shape[2]
    w_gdn = n_hg * HEAD_DIM
    w_fox = n_hf * HEAD_DIM
    w_pool = state_pool.shape[-1]
    pool_buf = state_pool.shape[2]
    conv_hist = state_conv.shape[2]
    past_len = page_table.shape[1] * page
    C = GDN_CHUNK
    assert dec_seq == 1 and seq % LANES == 0 and conv_hist == 3 and pool_buf == POOL_HIST - 1
    assert bs <= LANES

    tm_p = min(512, seq)
    tq = min(512, seq)
    tl = min(512, seq)
    pps = max(p for p in (32, 16, 8, 4, 2, 1) if page_table.shape[1] % p == 0)
    g_last = final_norm_g[None, :]

    xp = x_prompt.reshape(bp * seq, d)
    xs = x_sample.reshape(bs, d)
    cache_kt = jnp.transpose(cache_k, (0, 1, 3, 4, 2)).reshape(depth, n_pool, w_fox, page)
    cache_vt = jnp.transpose(cache_v, (0, 1, 3, 4, 2)).reshape(depth, n_pool, w_fox, page)
    cache_lf_t = jnp.swapaxes(cache_logf, 2, 3)
    w_in_t = jnp.transpose(w_in, (2, 0, 1))

    st_p, st_s = [], []
    for l in range(depth):
        lw = _layer_weights(l, w_gdn, w_fox, w_pool, norm_mix_g, w_in_t, conv_w, a_log, dt_bias,
                            gdn_norm_g, fox_bf, pool_w, pool_scale, w_out, norm_ffn_g, w_gate_up, w_down)
        final = l == depth - 1

        qkv, z, fq, fkt, fvt, pu, gt, cum, lft, cumt = _in_proj(
            xp.reshape(bp, seq, d), lw["g_mix"], lw["w_t"], lw["gp"], lw["seg"], tm_p)
        o_a, s_fin = _gdn(qkv, z, gt, lw["conv_w"], jnp.zeros((bp, 8, 3 * w_gdn), f32),
                          jnp.zeros((bp, n_hg, HEAD_DIM, HEAD_DIM), f32), lw["gdn_g"], C)
        o_f = _fox_prompt(fq, fkt, fvt, cum, cumt, tq)
        o_c = _pool(pu, jnp.zeros((bp, POOL_HIST, w_pool), f32), lw["w_bd"], lw["pool_scale"], 0, tl)
        xp = _tail(xp, o_a.reshape(bp * seq, w_gdn), o_f.reshape(bp * seq, w_fox),
                   o_c.reshape(bp * seq, w_pool), lw["wo"], lw["g_ffn"],
                   lw["wg"], lw["wu"], lw["wd"], g_last, final, tm_p)
        heads_last = lambda a: jnp.transpose(a.reshape(bp, n_hf, HEAD_DIM, seq), (0, 3, 1, 2))
        st_p.append((heads_last(fkt), heads_last(fvt), jnp.swapaxes(lft[:, :n_hf, :], 1, 2), s_fin,
                     qkv[:, seq - conv_hist:], pu[:, seq - pool_buf:]))

        xs_rows = jnp.pad(xs, ((0, LANES - bs), (0, 0)))[None]
        qkv, z, fq, fkt, fvt, pu, gt, _, _, _ = _in_proj(xs_rows, lw["g_mix"], lw["w_t"], lw["gp"],
                                                         lw["seg"], LANES)
        qkv, z, fq, pu, gt = (a[0, :bs] for a in (qkv, z, fq, pu, gt))
        fk, fv = (jnp.swapaxes(a[0, :, :bs], 0, 1) for a in (fkt, fvt))
        pad_c = lambda a: jnp.pad(a[:, None, :], ((0, 0), (0, C - 1), (0, 0)))
        conv0 = jnp.pad(state_conv[l], ((0, 0), (8 - conv_hist, 0), (0, 0)))
        o_a, s_fin = _gdn(pad_c(qkv), pad_c(z), pad_c(gt), lw["conv_w"], conv0, state_gdn[l],
                          lw["gdn_g"], 1)
        o_f = _fox_sample(page_table, fq[:, None, :], fk[:, None, :], fv[:, None, :], gt[:, None, :],
                          cache_kt, cache_vt, cache_lf_t, l, pps)
        hist0 = jnp.pad(state_pool[l], ((0, 0), (POOL_HIST - pool_buf, 0), (0, 0)))
        o_c = _pool(jnp.pad(pu[:, None, :], ((0, 0), (0, 7), (0, 0))), hist0, lw["w_bd"],
                    lw["pool_scale"], past_len, 8)
        xs = _tail(xs, o_a[:, 0], o_f[:, 0], o_c[:, 0], lw["wo"], lw["g_ffn"],
                   lw["wg"], lw["wu"], lw["wd"], g_last, final, bs)
        st_s.append((fk.reshape(bs, 1, n_hf, HEAD_DIM), fv.reshape(bs, 1, n_hf, HEAD_DIM),
                     gt[:, None, GATE_LOGF:GATE_LOGF + n_hf], s_fin,
                     jnp.concatenate([state_conv[l][:, 1:], qkv[:, None, :]], axis=1),
                     jnp.concatenate([state_pool[l][:, 1:], pu[:, None, :]], axis=1)))

    outs = [xp.reshape(bp, seq, d), xs.reshape(bs, 1, d)]
    for st in (st_p, st_s):
        for i in range(6):
            outs.append(jnp.stack([s[i] for s in st]))
    return tuple(outs)
```

```python
import functools

import jax
import jax.numpy as jnp
from jax import lax
from jax.experimental import pallas as pl
from jax.experimental.pallas import tpu as pltpu

HEAD_DIM = 64
POOL_WINDOWS = (2, 4, 8, 16)
POOL_HIST = 16
GDN_CHUNK = 64
SOLVE_BLOCK = 8
RMS_EPS = 1e-6
L2_EPS = 1e-6
LANES = 128
VMEM_LIMIT = 56 * 1024 * 1024
HI = lax.Precision.HIGHEST

GATE_BETA = 0
GATE_G = 6
GATE_LOGF = 12


def _cparams(sem):
    return pltpu.CompilerParams(dimension_semantics=sem, vmem_limit_bytes=VMEM_LIMIT)


def _const_spec(shape):
    nd = len(shape)
    return pl.BlockSpec(shape, lambda *_: (0,) * nd, pipeline_mode=pl.Buffered(1))


def _softplus(x):
    return jnp.maximum(x, 0.0) + jnp.log1p(jnp.exp(-jnp.abs(x)))


def _silu(x):
    return x * jax.nn.sigmoid(x)


def _dot_nt(a, b, precision=None):
    return lax.dot_general(a, b, (((1,), (1,)), ((), ())), precision=precision,
                           preferred_element_type=jnp.float32)


def _dot(a, b, precision=None):
    return jnp.dot(a, b, precision=precision, preferred_element_type=jnp.float32)


def _in_proj_kernel(x_ref, g_ref, wt_ref, gp_ref, qkv_ref, z_ref, fq_ref, fkt_ref, fvt_ref, pu_ref,
                    gt_ref, cum_ref, lft_ref, cumt_ref, carry_ref, *, seg, tm):
    x = x_ref[0]
    h = x * lax.rsqrt(jnp.mean(x * x, axis=-1, keepdims=True) + RMS_EPS) * g_ref[...]
    hb = h.astype(jnp.bfloat16)
    for name, o_ref in (("qkv", qkv_ref), ("z", z_ref), ("fq", fq_ref), ("pu", pu_ref)):
        r0, r1 = seg[name]
        o_ref[0] = _dot_nt(hb, wt_ref[r0:r1, :])
    for name, o_ref in (("fk", fkt_ref), ("fv", fvt_ref)):
        r0, r1 = seg[name]
        o_ref[0] = _dot_nt(wt_ref[r0:r1, :], hb)
    r0, r1 = seg["gates"]
    raw = _dot_nt(hb, wt_ref[r0:r1, :])
    lane = lax.broadcasted_iota(jnp.int32, raw.shape, 1)
    a_log = gp_ref[0:1, :]
    shifted = raw + gp_ref[1:2, :]
    beta = jax.nn.sigmoid(raw)
    g = -jnp.exp(a_log) * _softplus(shifted)
    logf = -_softplus(-shifted)
    gt = jnp.where(lane < GATE_G, beta,
                   jnp.where(lane < GATE_LOGF, g,
                             jnp.where(lane < GATE_LOGF + 6, logf, 0.0)))
    gt_ref[0] = gt

    @pl.when(pl.program_id(1) == 0)
    def _():
        carry_ref[...] = jnp.zeros_like(carry_ref)

    r = lax.broadcasted_iota(jnp.int32, (LANES, LANES), 0)
    c = lax.broadcasted_iota(jnp.int32, (LANES, LANES), 1)
    tril = (c <= r).astype(jnp.float32)
    carry = carry_ref[...]
    for s in range(tm // LANES):
        rows = slice(s * LANES, (s + 1) * LANES)
        cs = _dot(tril, gt[rows], HI) + carry
        cum_ref[0, rows, :] = cs
        cumt_ref[0, :, rows] = cs.T[GATE_LOGF:GATE_LOGF + 8, :]
        lft_ref[0, :, rows] = gt[rows].T[GATE_LOGF:GATE_LOGF + 8, :]
        carry = cs[LANES - 1:LANES, :]
    carry_ref[...] = carry


def _in_proj(x, g_mix, w_t, gate_params, seg, tm):
    b, l, d = x.shape
    row = lambda w: (jax.ShapeDtypeStruct((b, l, w), jnp.float32),
                     pl.BlockSpec((1, tm, w), lambda i, t: (i, t, 0)))
    col = lambda w: (jax.ShapeDtypeStruct((b, w, l), jnp.float32),
                     pl.BlockSpec((1, w, tm), lambda i, t: (i, 0, t)))
    width = lambda name: seg[name][1] - seg[name][0]
    outs = [row(width("qkv")), row(width("z")), row(width("fq")), col(width("fk")), col(width("fv")),
            row(width("pu")), row(LANES), row(LANES), col(8), col(8)]
    return pl.pallas_call(
        functools.partial(_in_proj_kernel, seg=seg, tm=tm),
        grid=(b, l // tm),
        in_specs=[pl.BlockSpec((1, tm, d), lambda i, t: (i, t, 0)), _const_spec((1, d)),
                  _const_spec(w_t.shape), _const_spec((8, LANES))],
        out_specs=[o[1] for o in outs],
        out_shape=[o[0] for o in outs],
        scratch_shapes=[pltpu.VMEM((1, LANES), jnp.float32)],
        compiler_params=_cparams(("arbitrary", "arbitrary")),
    )(x, g_mix, w_t, gate_params)


def _forward_substitute(a_blocks, x_blocks, lo, hi):
    for j in range(lo, hi - 1):
        for a_sys, x_sys in zip(a_blocks, x_blocks):
            row = jnp.broadcast_to(x_sys[j // 8][j % 8:j % 8 + 1, :], x_sys[0].shape)
            for rr in range((j + 1) // 8, hi // 8):
                x_sys[rr] = x_sys[rr] - a_sys[rr][:, j:j + 1] * row


def _solve_unit_lower(a_mats, rhs, n):
    bs = SOLVE_BLOCK
    a_blocks = [[a[r * 8:(r + 1) * 8, :] for r in range(n // 8)] for a in a_mats]
    x_blocks = [[x[r * 8:(r + 1) * 8, :] for r in range(n // 8)] for x in rhs]
    for lo in range(0, n, bs):
        if lo > 0:
            for a, x_sys in zip(a_mats, x_blocks):
                done = jnp.concatenate(x_sys[:lo // 8], axis=0)
                cur = jnp.concatenate(x_sys[lo // 8:(lo + bs) // 8], axis=0)
                cur = cur - _dot(a[lo:lo + bs, :lo], done, HI)
                x_sys[lo // 8:(lo + bs) // 8] = [cur[r * 8:(r + 1) * 8, :] for r in range(bs // 8)]
        _forward_substitute(a_blocks, x_blocks, lo, lo + bs)
    return [jnp.concatenate(x_sys, axis=0) for x_sys in x_blocks]


def _gdn_prep_kernel(qkv_ref, gt_ref, cw_ref, conv0_ref, mats_ref, qd_ref, glast_ref, hist_ref,
                     *, n_heads, valid, chunks):
    c = pl.program_id(1)
    C = GDN_CHUNK
    D = HEAD_DIM
    W = n_heads * D

    @pl.when(c == 0)
    def _():
        hist_ref[...] = conv0_ref[0]

    x = qkv_ref[0]
    rows = chunks * C
    ext = jnp.concatenate([hist_ref[...], x], axis=0)
    conv_all = ext[5:5 + rows] * cw_ref[0:1, :]
    for j in range(1, 4):
        conv_all = conv_all + ext[5 + j:5 + j + rows] * cw_ref[j:j + 1, :]
    conv_all = _silu(conv_all)
    hist_ref[...] = x[rows - 8:rows]

    ri = lax.broadcasted_iota(jnp.int32, (C, C), 0)
    ci = lax.broadcasted_iota(jnp.int32, (C, C), 1)
    tri = ci <= ri
    strict = ci < ri

    pending = []
    for n in range(chunks):
        conv = conv_all[n * C:(n + 1) * C]
        gt = gt_ref[0, n * C:(n + 1) * C, :]
        if valid < C:
            live = lax.broadcasted_iota(jnp.int32, (C, 1), 0) < valid
            conv = jnp.where(live, conv, 0.0)
            gt = jnp.where(live, gt, 0.0)
        g_cum = _dot(tri.astype(jnp.float32), gt, HI)
        g_cum_t = g_cum.T
        glast_ref[0, n] = jnp.broadcast_to(g_cum[C - 1:C, :], (8, LANES))
        a_mats, rhs_all, side = [], [], []
        for h in range(n_heads):
            q = conv[:, h * D:(h + 1) * D]
            k = conv[:, W + h * D:W + (h + 1) * D]
            v = conv[:, 2 * W + h * D:2 * W + (h + 1) * D]
            q = q * lax.rsqrt(jnp.sum(q * q, axis=-1, keepdims=True) + L2_EPS) * (D ** -0.5)
            k = k * lax.rsqrt(jnp.sum(k * k, axis=-1, keepdims=True) + L2_EPS)
            beta = gt[:, GATE_BETA + h:GATE_BETA + h + 1]
            gc = g_cum[:, GATE_G + h:GATE_G + h + 1]
            gr = g_cum_t[GATE_G + h:GATE_G + h + 1, :]
            g_last = gc[C - 1:C, :]
            decay = jnp.where(tri, jnp.exp(jnp.where(tri, gc - gr, 0.0)), 0.0)
            e_g = jnp.exp(gc)
            kb = k * beta
            a_mats.append(jnp.where(strict, _dot_nt(kb, k) * decay, 0.0))
            rhs_all.append(jnp.concatenate([v * beta, kb * e_g], axis=1))
            side.append((_dot_nt(q, k) * decay, q * e_g, (k * jnp.exp(g_last - gc)).T))
        pending.append((a_mats, rhs_all, side))

    sols = _solve_unit_lower([a for p in pending for a in p[0]], [r for p in pending for r in p[1]], C)
    for n, (_, _, side) in enumerate(pending):
        for h in range(n_heads):
            sol = sols[n * n_heads + h]
            qk, qd, kd_t = side[h]
            mats_ref[0, n, h, 0:C, :] = _dot(qk, sol)
            mats_ref[0, n, h, C:2 * C, :] = _dot(kd_t, sol)
            qd_ref[0, n, h] = qd


def _gdn_scan_kernel(mats_ref, qd_ref, glast_ref, z_ref, s0_ref, gn_ref, o_ref, s_ref,
                     *, n_heads, chunks):
    C = GDN_CHUNK
    D = HEAD_DIM

    @pl.when(pl.program_id(1) == 0)
    def _():
        s_ref[...] = s0_ref[...]

    for n in range(chunks):
        outs = []
        for h in range(n_heads):
            g_last = glast_ref[0, n][0:1, GATE_G + h:GATE_G + h + 1]
            s_prev = s_ref[0, h]
            s_low = jnp.concatenate([jnp.zeros((D, D), jnp.float32), s_prev], axis=0)
            moved = _dot(mats_ref[0, n, h], s_low)
            o = _dot(qd_ref[0, n, h], s_prev) - moved[:C] + mats_ref[0, n, h, 0:C, 0:D]
            s_ref[0, h] = s_prev * jnp.exp(g_last) - moved[C:] + mats_ref[0, n, h, C:2 * C, 0:D]
            o = o * lax.rsqrt(jnp.mean(o * o, axis=-1, keepdims=True) + RMS_EPS) * gn_ref[...]
            outs.append(o * _silu(z_ref[0, n * C:(n + 1) * C, h * D:(h + 1) * D]))
        o_ref[0, n * C:(n + 1) * C, :] = jnp.concatenate(outs, axis=1)


def _gdn(qkv, z, gt, conv_w, conv0, s0, gdn_g, valid):
    b, l, w3 = qkv.shape
    w = w3 // 3
    n_heads = w // HEAD_DIM
    C = GDN_CHUNK
    D = HEAD_DIM
    nc = l // C
    f32 = jnp.float32
    per_b = lambda shape: pl.BlockSpec((1,) + shape, lambda i, j: (i,) + (0,) * len(shape))
    chunk = lambda n, shape: pl.BlockSpec((1, n) + shape, lambda i, j: (i, j) + (0,) * len(shape))
    tok = lambda n, width: pl.BlockSpec((1, n * C, width), lambda i, j: (i, j, 0))
    cpp = max(n for n in (2, 1) if nc % n == 0)
    mats, qd, glast = pl.pallas_call(
        functools.partial(_gdn_prep_kernel, n_heads=n_heads, valid=valid, chunks=cpp),
        grid=(b, nc // cpp),
        in_specs=[tok(cpp, w3), tok(cpp, LANES), _const_spec((4, w3)), per_b((8, w3))],
        out_specs=[chunk(cpp, (n_heads, 2 * C, 2 * D)), chunk(cpp, (n_heads, C, D)), chunk(cpp, (8, LANES))],
        out_shape=[jax.ShapeDtypeStruct((b, nc, n_heads, 2 * C, 2 * D), f32),
                   jax.ShapeDtypeStruct((b, nc, n_heads, C, D), f32),
                   jax.ShapeDtypeStruct((b, nc, 8, LANES), f32)],
        scratch_shapes=[pltpu.VMEM((8, w3), f32)],
        compiler_params=_cparams(("arbitrary", "arbitrary")),
    )(qkv, gt, conv_w, conv0)
    cps = max(n for n in (8, 4, 2, 1) if nc % n == 0)
    return pl.pallas_call(
        functools.partial(_gdn_scan_kernel, n_heads=n_heads, chunks=cps),
        grid=(b, nc // cps),
        in_specs=[chunk(cps, (n_heads, 2 * C, 2 * D)), chunk(cps, (n_heads, C, D)), chunk(cps, (8, LANES)),
                  tok(cps, w), per_b((n_heads, D, D)), _const_spec((1, D))],
        out_specs=[tok(cps, w), per_b((n_heads, D, D))],
        out_shape=[jax.ShapeDtypeStruct((b, l, w), f32), jax.ShapeDtypeStruct((b, n_heads, D, D), f32)],
        compiler_params=_cparams(("arbitrary", "arbitrary")),
    )(mats, qd, glast, z, s0, gdn_g)


def _split3(x):
    hi = x.astype(jnp.bfloat16).astype(jnp.float32)
    mid = (x - hi).astype(jnp.bfloat16).astype(jnp.float32)
    return hi, mid, (x - hi) - mid


def _fox_prompt_kernel(q_ref, kt_ref, vt_ref, cum_ref, cum_t_ref, o_ref, kta_ref, vtb_ref, m_ref, l_ref,
                       acc_ref, *, tq):
    hp = pl.program_id(1)
    qi = pl.program_id(2)
    bf = jnp.bfloat16
    f32 = jnp.float32
    seq = kt_ref.shape[2]
    log2e = 1.4426950408889634
    lane = lax.broadcasted_iota(jnp.int32, (tq, LANES), 1)
    low = lane < HEAD_DIM

    @pl.when(qi == 0)
    def _():
        row = lax.broadcasted_iota(jnp.int32, (LANES, tq), 0)
        sub = lax.broadcasted_iota(jnp.int32, (8, tq), 0)
        for t in range(seq // tq):
            cols = slice(t * tq, (t + 1) * tq)
            kt = kt_ref[0, :, cols]
            vtb_ref[:, cols] = vt_ref[0, :, cols].astype(bf)
            c_cols = cum_t_ref[0, :, cols]
            for hh in range(2):
                c_col = jnp.sum(jnp.where(sub == 2 * hp + hh, c_cols, 0.0), axis=0, keepdims=True)
                hi, mid, lo = _split3(-log2e * c_col)
                r = row - (HEAD_DIM if hh == 0 else 0)
                own = (row < HEAD_DIM) == (hh == 0)
                aug = jnp.where(r == 3, hi, jnp.where(r == 4, mid, jnp.where(r == 5, lo, 0.0)))
                aug = jnp.where((r >= 0) & (r < 3), 1.0, aug)
                kta_ref[hh, :, cols] = jnp.where(own, kt, aug).astype(bf)

    q = q_ref[0] * (HEAD_DIM ** -0.5 * log2e)
    cum = cum_ref[0]
    head_lane = lax.broadcasted_iota(jnp.int32, cum.shape, 1) - GATE_LOGF - 2 * hp
    q_heads = []
    for hh in range(2):
        c_row = jnp.sum(jnp.where(head_lane == hh, cum, 0.0), axis=1, keepdims=True)
        hi, mid, lo = _split3(log2e * c_row)
        r = lane - (HEAD_DIM if hh == 0 else 0)
        own = low == (hh == 0)
        aug = jnp.where(r == 0, hi, jnp.where(r == 1, mid, jnp.where(r == 2, lo, 0.0)))
        aug = jnp.where((r >= 3) & (r < 6), 1.0, aug)
        q_heads.append(jnp.where(own, q, aug).astype(bf))

    m_ref[...] = jnp.full(m_ref.shape, -jnp.inf, f32)
    l_ref[...] = jnp.zeros(l_ref.shape, f32)
    acc_ref[...] = jnp.zeros(acc_ref.shape, f32)

    def tile(ki, masked):
        ks = pl.multiple_of(ki * tq, tq)
        vt = vtb_ref[:, pl.ds(ks, tq)]
        for hh in range(2):
            s = _dot(q_heads[hh], kta_ref[hh, :, pl.ds(ks, tq)])
            if masked:
                r = lax.broadcasted_iota(jnp.int32, s.shape, 0)
                cc = lax.broadcasted_iota(jnp.int32, s.shape, 1)
                s = jnp.where(cc <= r, s, -jnp.inf)
            m_prev = m_ref[hh]
            m_new = jnp.maximum(m_prev, jnp.max(s, axis=1, keepdims=True))
            p = jnp.exp2(s - jnp.tile(m_new, (1, tq // LANES)))
            alpha = jnp.exp2(m_prev - m_new)
            l_ref[hh] = alpha * l_ref[hh] + jnp.sum(p, axis=1, keepdims=True)
            acc_ref[hh] = alpha * acc_ref[hh] + _dot_nt(p.astype(bf), vt)
            m_ref[hh] = m_new

    def body(ki, carry):
        tile(ki, False)
        return carry

    lax.fori_loop(0, qi, body, 0)
    tile(qi, True)
    o_ref[0] = jnp.where(low, acc_ref[0] / l_ref[0], acc_ref[1] / l_ref[1])


def _fox_prompt(fq, fkt, fvt, cum, cum_t, tq):
    b, l, w = fq.shape
    n_pairs = w // LANES
    q_spec = pl.BlockSpec((1, tq, LANES), lambda i, hp, qi: (i, qi, hp))
    kv_spec = pl.BlockSpec((1, LANES, l), lambda i, hp, qi: (i, hp, 0))
    return pl.pallas_call(
        functools.partial(_fox_prompt_kernel, tq=tq),
        grid=(b, n_pairs, l // tq),
        in_specs=[q_spec, kv_spec, kv_spec,
                  pl.BlockSpec((1, tq, LANES), lambda i, hp, qi: (i, qi, 0)),
                  pl.BlockSpec((1, 8, l), lambda i, hp, qi: (i, 0, 0))],
        out_specs=q_spec,
        out_shape=jax.ShapeDtypeStruct((b, l, w), jnp.float32),
        scratch_shapes=[pltpu.VMEM((2, LANES, l), jnp.bfloat16), pltpu.VMEM((LANES, l), jnp.bfloat16),
                        pltpu.VMEM((2, tq, LANES), jnp.float32), pltpu.VMEM((2, tq, LANES), jnp.float32),
                        pltpu.VMEM((2, tq, LANES), jnp.float32)],
        compiler_params=_cparams(("arbitrary", "arbitrary", "arbitrary")),
    )(fq, fkt, fvt, cum, cum_t)


def _fox_sample_kernel(pt_ref, q_ref, kn_ref, vn_ref, gt_ref, *refs, pages_per_step, n_heads):
    del pt_ref
    pp = pages_per_step
    k_refs, v_refs, lf_refs = refs[:pp], refs[pp:2 * pp], refs[2 * pp:3 * pp]
    o_ref, m_ref, l_ref, run_ref, acc_ref = refs[3 * pp:]
    j = pl.program_id(1)
    w = n_heads * HEAD_DIM
    page = k_refs[0].shape[-1]

    sub = lax.broadcasted_iota(jnp.int32, (8, w), 0)
    own = lax.broadcasted_iota(jnp.int32, (8, w), 1) // HEAD_DIM == sub
    qb = jnp.where(own, q_ref[0] * (HEAD_DIM ** -0.5), 0.0)

    @pl.when(j == 0)
    def _():
        m_ref[...] = jnp.sum(qb * kn_ref[0], axis=1, keepdims=True)
        l_ref[...] = jnp.ones_like(l_ref)
        acc_ref[...] = jnp.broadcast_to(vn_ref[0], acc_ref.shape)
        gt = gt_ref[0]
        pick = (lax.broadcasted_iota(jnp.int32, (8, LANES), 1) - GATE_LOGF
                == lax.broadcasted_iota(jnp.int32, (8, LANES), 0))
        run_ref[...] = jnp.sum(jnp.where(pick, gt, 0.0), axis=1, keepdims=True)

    r = lax.broadcasted_iota(jnp.int32, (page, page), 0)
    c = lax.broadcasted_iota(jnp.int32, (page, page), 1)
    later = (r > c).astype(jnp.float32)
    qbb = qb.astype(jnp.bfloat16)
    run = run_ref[...]
    scores = []
    pad = jnp.zeros((8 - n_heads, page), jnp.float32)
    lf_all = jnp.concatenate([x for i in range(pp) for x in (lf_refs[i][0, 0], pad)], axis=0)
    suffix = _dot(lf_all, later, HI)
    totals = jnp.sum(lf_all, axis=1, keepdims=True)
    for i in range(pp):
        s = _dot(qbb, k_refs[i][0, 0].astype(jnp.bfloat16))
        scores.append(s + run + suffix[8 * i:8 * (i + 1)])
        run = run + totals[8 * i:8 * (i + 1)]
    run_ref[...] = run
    s_all = jnp.concatenate(scores, axis=1)
    m_prev = m_ref[...]
    m_new = jnp.maximum(m_prev, jnp.max(s_all, axis=1, keepdims=True))
    p = jnp.exp(s_all - m_new)
    alpha = jnp.exp(m_prev - m_new)
    l_ref[...] = alpha * l_ref[...] + jnp.sum(p, axis=1, keepdims=True)
    pv = _dot_nt(p[:, :page].astype(jnp.bfloat16), v_refs[0][0, 0].astype(jnp.bfloat16))
    for i in range(1, pp):
        pv = pv + _dot_nt(p[:, i * page:(i + 1) * page].astype(jnp.bfloat16),
                          v_refs[i][0, 0].astype(jnp.bfloat16))
    acc_ref[...] = alpha * acc_ref[...] + pv
    m_ref[...] = m_new

    @pl.when(j == pl.num_programs(1) - 1)
    def _():
        o = jnp.where(own, acc_ref[...] / l_ref[...], 0.0)
        o_ref[0] = jnp.sum(o, axis=0, keepdims=True)


def _fox_sample(page_table, fq, fk, fv, gt, cache_kt, cache_vt, cache_lf_t, layer, pages_per_step):
    b, _, w = fq.shape
    n_heads = w // HEAD_DIM
    page = cache_kt.shape[-1]
    n_pages = page_table.shape[1]
    pp = pages_per_step

    def page_map(i):
        return lambda bi, j, pt: (layer, pt[bi, n_pages - 1 - (j * pp + i)], 0, 0)

    tok = lambda width: pl.BlockSpec((1, 1, width), lambda bi, j, pt: (bi, 0, 0))
    in_specs = [tok(w), tok(w), tok(w), tok(LANES)]
    in_specs += [pl.BlockSpec((1, 1, w, page), page_map(i)) for i in range(pp)]
    in_specs += [pl.BlockSpec((1, 1, w, page), page_map(i)) for i in range(pp)]
    in_specs += [pl.BlockSpec((1, 1, n_heads, page), page_map(i)) for i in range(pp)]
    grid_spec = pltpu.PrefetchScalarGridSpec(
        num_scalar_prefetch=1, grid=(b, n_pages // pp), in_specs=in_specs, out_specs=tok(w),
        scratch_shapes=[pltpu.VMEM((8, 1), jnp.float32), pltpu.VMEM((8, 1), jnp.float32),
                        pltpu.VMEM((8, 1), jnp.float32), pltpu.VMEM((8, w), jnp.float32)])
    return pl.pallas_call(
        functools.partial(_fox_sample_kernel, pages_per_step=pp, n_heads=n_heads),
        grid_spec=grid_spec,
        out_shape=jax.ShapeDtypeStruct((b, 1, w), jnp.float32),
        compiler_params=_cparams(("arbitrary", "arbitrary")),
    )(page_table, fq, fk, fv, gt, *([cache_kt] * pp), *([cache_vt] * pp), *([cache_lf_t] * pp))


def _pool_kernel(u_ref, hist0_ref, w_ref, scale_ref, o_ref, hist_ref, *, pos0, tl):
    t = pl.program_id(1)

    @pl.when(t == 0)
    def _():
        hist_ref[...] = hist0_ref[0]

    u = u_ref[0]
    ext = jnp.concatenate([hist_ref[...], u], axis=0)
    hist_ref[...] = ext[tl:tl + POOL_HIST]
    sums = []
    acc = ext
    for step in (1, 2, 4, 8):
        acc = acc + pltpu.roll(acc, step, axis=0)
        sums.append(acc[POOL_HIST:])
    width = u.shape[1]
    group = lax.broadcasted_iota(jnp.int32, (tl, width), 1) // (width // len(POOL_WINDOWS))
    wsum = jnp.where(group == 0, sums[0], jnp.where(group == 1, sums[1],
                                                    jnp.where(group == 2, sums[2], sums[3])))
    window = jnp.where(group == 0, POOL_WINDOWS[0],
                       jnp.where(group == 1, POOL_WINDOWS[1],
                                 jnp.where(group == 2, POOL_WINDOWS[2], POOL_WINDOWS[3])))
    pos = pos0 + t * tl + lax.broadcasted_iota(jnp.int32, (tl, width), 0)
    cnt = jnp.minimum(pos + 1, window).astype(jnp.float32)
    d = wsum / cnt - u
    o_ref[0] = _dot(d.astype(jnp.bfloat16), w_ref[...]) * scale_ref[...]


def _pool(u, hist0, w_bd, scale, pos0, tl):
    b, l, w = u.shape
    return pl.pallas_call(
        functools.partial(_pool_kernel, pos0=pos0, tl=tl),
        grid=(b, l // tl),
        in_specs=[pl.BlockSpec((1, tl, w), lambda i, t: (i, t, 0)),
                  pl.BlockSpec((1, POOL_HIST, w), lambda i, t: (i, 0, 0)),
                  _const_spec((w, w)), _const_spec((1, w))],
        out_specs=pl.BlockSpec((1, tl, w), lambda i, t: (i, t, 0)),
        out_shape=jax.ShapeDtypeStruct((b, l, w), jnp.float32),
        scratch_shapes=[pltpu.VMEM((POOL_HIST, w), jnp.float32)],
        compiler_params=_cparams(("arbitrary", "arbitrary")),
    )(u, hist0, w_bd, scale)


def _tail_kernel(x_ref, oa_ref, of_ref, oc_ref, wo_ref, gf_ref, wg_ref, wu_ref, wd_ref, gl_ref, o_ref,
                 *, ff_chunks, final_norm):
    bf = jnp.bfloat16
    mixed = jnp.concatenate([oa_ref[...], of_ref[...], oc_ref[...]], axis=1).astype(bf)
    x = x_ref[...] + _dot(mixed, wo_ref[...])
    h = (x * lax.rsqrt(jnp.mean(x * x, axis=-1, keepdims=True) + RMS_EPS) * gf_ref[...]).astype(bf)
    for c0, c1 in ff_chunks:
        act = _silu(_dot(h, wg_ref[:, c0:c1])) * _dot(h, wu_ref[:, c0:c1])
        x = x + _dot(act.astype(bf), wd_ref[c0:c1, :])
    if final_norm:
        x = x * lax.rsqrt(jnp.mean(x * x, axis=-1, keepdims=True) + RMS_EPS) * gl_ref[...]
    o_ref[...] = x


def _tail(x2d, oa, of, oc, wo, g_ffn, wg, wu, wd, g_last, final_norm, tm):
    n, d = x2d.shape
    d_ff = wg.shape[1]
    step = 1024
    ff_chunks = tuple((c0, min(c0 + step, d_ff)) for c0 in range(0, d_ff, step))
    row_spec = lambda w: pl.BlockSpec((tm, w), lambda i: (i, 0))
    weights = (wo, g_ffn, wg, wu, wd, g_last)
    return pl.pallas_call(
        functools.partial(_tail_kernel, ff_chunks=ff_chunks, final_norm=final_norm),
        grid=(n // tm,),
        in_specs=[row_spec(d), row_spec(oa.shape[1]), row_spec(of.shape[1]), row_spec(oc.shape[1])]
        + [_const_spec(a.shape) for a in weights],
        out_specs=row_spec(d),
        out_shape=jax.ShapeDtypeStruct((n, d), jnp.float32),
        compiler_params=_cparams(("arbitrary",)),
    )(x2d, oa, of, oc, *weights)


def _layer_weights(l, w_gdn, w_fox, w_pool, norm_mix_g, w_in_t, conv_w, a_log, dt_bias, gdn_norm_g,
                   fox_bf, pool_w, pool_scale, w_out, norm_ffn_g, w_gate_up, w_down):
    bf = jnp.bfloat16
    n_hg = w_gdn // HEAD_DIM
    n_hf = w_fox // HEAD_DIM
    d_ff = w_down.shape[1]
    o = 0
    src = {}
    for name, width in (("qkv", 3 * w_gdn), ("z", w_gdn), ("beta", n_hg), ("alpha", n_hg),
                        ("fq", w_fox), ("fk", w_fox), ("fv", w_fox), ("fgate", n_hf), ("pu", w_pool)):
        src[name] = (o, o + width)
        o += width
    wl = w_in_t[:, l, :]
    rows = lambda name: wl[src[name][0]:src[name][1]]
    gate_rows = jnp.concatenate([rows("beta"), rows("alpha"), rows("fgate")], axis=0)
    gate_rows = jnp.pad(gate_rows, ((0, LANES - gate_rows.shape[0]), (0, 0)))
    order = ("qkv", "z", "fq", "fk", "fv", "pu")
    w_t = jnp.concatenate([rows(n) for n in order] + [gate_rows], axis=0).astype(bf)
    seg, o = {}, 0
    for name in order:
        width = src[name][1] - src[name][0]
        seg[name] = (o, o + width)
        o += width
    seg["gates"] = (o, o + LANES)
    gp = jnp.zeros((8, LANES), jnp.float32)
    gp = gp.at[0, GATE_G:GATE_G + n_hg].set(a_log[l])
    gp = gp.at[1, GATE_G:GATE_G + n_hg].set(dt_bias[l])
    gp = gp.at[1, GATE_LOGF:GATE_LOGF + n_hf].set(fox_bf[l])
    n_groups, pg, _ = pool_w[l].shape
    w_bd = jnp.zeros((w_pool, w_pool), jnp.float32)
    for gi in range(n_groups):
        w_bd = w_bd.at[gi * pg:(gi + 1) * pg, gi * pg:(gi + 1) * pg].set(pool_w[l, gi])
    return dict(
        g_mix=norm_mix_g[l][None, :], w_t=w_t, seg=seg, gp=gp, conv_w=conv_w[l],
        gdn_g=gdn_norm_g[l][None, :], w_bd=w_bd.astype(bf), pool_scale=pool_scale[l][None, :],
        wo=w_out[l].astype(bf),
        g_ffn=norm_ffn_g[l][None, :], wg=w_gate_up[l][:, :d_ff].astype(bf),
        wu=w_gate_up[l][:, d_ff:].astype(bf), wd=w_down[l].astype(bf))


def kernel(x_prompt, x_sample, cache_k, cache_v, cache_logf, state_gdn, state_conv, state_pool,
           page_table, norm_mix_g, w_in, conv_w, a_log, dt_bias, gdn_norm_g, fox_bf, pool_w,
           pool_scale, w_out, norm_ffn_g, w_gate_up, w_down, final_norm_g):
    f32 = jnp.float32
    bp, seq, d = x_prompt.shape
    bs, dec_seq, _ = x_sample.shape
    depth, n_pool, page, n_hf, _ = cache_k.shape
    n_hg = state_gdn.shape[2]
    w_gdn = n_hg * HEAD_DIM
    w_fox = n_hf * HEAD_DIM
    w_pool = state_pool.shape[-1]
    pool_buf = state_pool.shape[2]
    conv_hist = state_conv.shape[2]
    past_len = page_table.shape[1] * page
    C = GDN_CHUNK
    assert dec_seq == 1 and seq % LANES == 0 and conv_hist == 3 and pool_buf == POOL_HIST - 1
    assert bs <= LANES

    tm_p = min(512, seq)
    tq = min(512, seq)
    tl = min(512, seq)
    pps = max(p for p in (32, 16, 8, 4, 2, 1) if page_table.shape[1] % p == 0)
    g_last = final_norm_g[None, :]

    xp = x_prompt.reshape(bp * seq, d)
    xs = x_sample.reshape(bs, d)
    cache_kt = jnp.transpose(cache_k, (0, 1, 3, 4, 2)).reshape(depth, n_pool, w_fox, page)
    cache_vt = jnp.transpose(cache_v, (0, 1, 3, 4, 2)).reshape(depth, n_pool, w_fox, page)
    cache_lf_t = jnp.swapaxes(cache_logf, 2, 3)
    w_in_t = jnp.transpose(w_in, (2, 0, 1))

    st_p, st_s = [], []
    for l in range(depth):
        lw = _layer_weights(l, w_gdn, w_fox, w_pool, norm_mix_g, w_in_t, conv_w, a_log, dt_bias,
                            gdn_norm_g, fox_bf, pool_w, pool_scale, w_out, norm_ffn_g, w_gate_up, w_down)
        final = l == depth - 1

        qkv, z, fq, fkt, fvt, pu, gt, cum, lft, cumt = _in_proj(
            xp.reshape(bp, seq, d), lw["g_mix"], lw["w_t"], lw["gp"], lw["seg"], tm_p)
        o_a, s_fin = _gdn(qkv, z, gt, lw["conv_w"], jnp.zeros((bp, 8, 3 * w_gdn), f32),
                          jnp.zeros((bp, n_hg, HEAD_DIM, HEAD_DIM), f32), lw["gdn_g"], C)
        o_f = _fox_prompt(fq, fkt, fvt, cum, cumt, tq)
        o_c = _pool(pu, jnp.zeros((bp, POOL_HIST, w_pool), f32), lw["w_bd"], lw["pool_scale"], 0, tl)
        xp = _tail(xp, o_a.reshape(bp * seq, w_gdn), o_f.reshape(bp * seq, w_fox),
                   o_c.reshape(bp * seq, w_pool), lw["wo"], lw["g_ffn"],
                   lw["wg"], lw["wu"], lw["wd"], g_last, final, tm_p)
        heads_last = lambda a: jnp.transpose(a.reshape(bp, n_hf, HEAD_DIM, seq), (0, 3, 1, 2))
        st_p.append((heads_last(fkt), heads_last(fvt), jnp.swapaxes(lft[:, :n_hf, :], 1, 2), s_fin,
                     qkv[:, seq - conv_hist:], pu[:, seq - pool_buf:]))

        xs_rows = jnp.pad(xs, ((0, LANES - bs), (0, 0)))[None]
        qkv, z, fq, fkt, fvt, pu, gt, _, _, _ = _in_proj(xs_rows, lw["g_mix"], lw["w_t"], lw["gp"],
                                                         lw["seg"], LANES)
        qkv, z, fq, pu, gt = (a[0, :bs] for a in (qkv, z, fq, pu, gt))
        fk, fv = (jnp.swapaxes(a[0, :, :bs], 0, 1) for a in (fkt, fvt))
        pad_c = lambda a: jnp.pad(a[:, None, :], ((0, 0), (0, C - 1), (0, 0)))
        conv0 = jnp.pad(state_conv[l], ((0, 0), (8 - conv_hist, 0), (0, 0)))
        o_a, s_fin = _gdn(pad_c(qkv), pad_c(z), pad_c(gt), lw["conv_w"], conv0, state_gdn[l],
                          lw["gdn_g"], 1)
        o_f = _fox_sample(page_table, fq[:, None, :], fk[:, None, :], fv[:, None, :], gt[:, None, :],
                          cache_kt, cache_vt, cache_lf_t, l, pps)
        hist0 = jnp.pad(state_pool[l], ((0, 0), (POOL_HIST - pool_buf, 0), (0, 0)))
        o_c = _pool(jnp.pad(pu[:, None, :], ((0, 0), (0, 7), (0, 0))), hist0, lw["w_bd"],
                    lw["pool_scale"], past_len, 8)
        xs = _tail(xs, o_a[:, 0], o_f[:, 0], o_c[:, 0], lw["wo"], lw["g_ffn"],
                   lw["wg"], lw["wu"], lw["wd"], g_last, final, bs)
        st_s.append((fk.reshape(bs, 1, n_hf, HEAD_DIM), fv.reshape(bs, 1, n_hf, HEAD_DIM),
                     gt[:, None, GATE_LOGF:GATE_LOGF + n_hf], s_fin,
                     jnp.concatenate([state_conv[l][:, 1:], qkv[:, None, :]], axis=1),
                     jnp.concatenate([state_pool[l][:, 1:], pu[:, None, :]], axis=1)))

    outs = [xp.reshape(bp, seq, d), xs.reshape(bs, 1, d)]
    for st in (st_p, st_s):
        for i in range(6):
            outs.append(jnp.stack([s[i] for s in st]))
    return tuple(outs)
```

```python
import functools

import jax
import jax.numpy as jnp
from jax import lax
from jax.experimental import pallas as pl
from jax.experimental.pallas import tpu as pltpu

HEAD_DIM = 64
POOL_WINDOWS = (2, 4, 8, 16)
POOL_HIST = 16
GDN_CHUNK = 64
SOLVE_BLOCK = 8
RMS_EPS = 1e-6
L2_EPS = 1e-6
LANES = 128
VMEM_LIMIT = 56 * 1024 * 1024
HI = lax.Precision.HIGHEST

GATE_BETA = 0
GATE_G = 6
GATE_LOGF = 12


def _cparams(sem):
    return pltpu.CompilerParams(dimension_semantics=sem, vmem_limit_bytes=VMEM_LIMIT)


def _const_spec(shape):
    nd = len(shape)
    return pl.BlockSpec(shape, lambda *_: (0,) * nd, pipeline_mode=pl.Buffered(1))


def _softplus(x):
    return jnp.maximum(x, 0.0) + jnp.log1p(jnp.exp(-jnp.abs(x)))


def _silu(x):
    return x * jax.nn.sigmoid(x)


def _dot_nt(a, b, precision=None):
    return lax.dot_general(a, b, (((1,), (1,)), ((), ())), precision=precision,
                           preferred_element_type=jnp.float32)


def _dot(a, b, precision=None):
    return jnp.dot(a, b, precision=precision, preferred_element_type=jnp.float32)


def _in_proj_kernel(x_ref, g_ref, wt_ref, gp_ref, qkv_ref, z_ref, fq_ref, fkt_ref, fvt_ref, pu_ref,
                    gt_ref, cum_ref, lft_ref, cumt_ref, carry_ref, *, seg, tm):
    x = x_ref[0]
    h = x * lax.rsqrt(jnp.mean(x * x, axis=-1, keepdims=True) + RMS_EPS) * g_ref[...]
    hb = h.astype(jnp.bfloat16)
    for name, o_ref in (("qkv", qkv_ref), ("z", z_ref), ("fq", fq_ref), ("pu", pu_ref)):
        r0, r1 = seg[name]
        o_ref[0] = _dot_nt(hb, wt_ref[r0:r1, :])
    for name, o_ref in (("fk", fkt_ref), ("fv", fvt_ref)):
        r0, r1 = seg[name]
        o_ref[0] = _dot_nt(wt_ref[r0:r1, :], hb)
    r0, r1 = seg["gates"]
    raw = _dot_nt(hb, wt_ref[r0:r1, :])
    lane = lax.broadcasted_iota(jnp.int32, raw.shape, 1)
    a_log = gp_ref[0:1, :]
    shifted = raw + gp_ref[1:2, :]
    beta = jax.nn.sigmoid(raw)
    g = -jnp.exp(a_log) * _softplus(shifted)
    logf = -_softplus(-shifted)
    gt = jnp.where(lane < GATE_G, beta,
                   jnp.where(lane < GATE_LOGF, g,
                             jnp.where(lane < GATE_LOGF + 6, logf, 0.0)))
    gt_ref[0] = gt

    @pl.when(pl.program_id(1) == 0)
    def _():
        carry_ref[...] = jnp.zeros_like(carry_ref)

    r = lax.broadcasted_iota(jnp.int32, (LANES, LANES), 0)
    c = lax.broadcasted_iota(jnp.int32, (LANES, LANES), 1)
    tril = (c <= r).astype(jnp.float32)
    carry = carry_ref[...]
    for s in range(tm // LANES):
        rows = slice(s * LANES, (s + 1) * LANES)
        cs = _dot(tril, gt[rows], HI) + carry
        cum_ref[0, rows, :] = cs
        cumt_ref[0, :, rows] = cs.T[GATE_LOGF:GATE_LOGF + 8, :]
        lft_ref[0, :, rows] = gt[rows].T[GATE_LOGF:GATE_LOGF + 8, :]
        carry = cs[LANES - 1:LANES, :]
    carry_ref[...] = carry


def _in_proj(x, g_mix, w_t, gate_params, seg, tm):
    b, l, d = x.shape
    row = lambda w: (jax.ShapeDtypeStruct((b, l, w), jnp.float32),
                     pl.BlockSpec((1, tm, w), lambda i, t: (i, t, 0)))
    col = lambda w: (jax.ShapeDtypeStruct((b, w, l), jnp.float32),
                     pl.BlockSpec((1, w, tm), lambda i, t: (i, 0, t)))
    width = lambda name: seg[name][1] - seg[name][0]
    outs = [row(width("qkv")), row(width("z")), row(width("fq")), col(width("fk")), col(width("fv")),
            row(width("pu")), row(LANES), row(LANES), col(8), col(8)]
    return pl.pallas_call(
        functools.partial(_in_proj_kernel, seg=seg, tm=tm),
        grid=(b, l // tm),
        in_specs=[pl.BlockSpec((1, tm, d), lambda i, t: (i, t, 0)), _const_spec((1, d)),
                  _const_spec(w_t.shape), _const_spec((8, LANES))],
        out_specs=[o[1] for o in outs],
        out_shape=[o[0] for o in outs],
        scratch_shapes=[pltpu.VMEM((1, LANES), jnp.float32)],
        compiler_params=_cparams(("arbitrary", "arbitrary")),
    )(x, g_mix, w_t, gate_params)


def _forward_substitute(a_blocks, x_blocks, lo, hi):
    for j in range(lo, hi - 1):
        for a_sys, x_sys in zip(a_blocks, x_blocks):
            row = jnp.broadcast_to(x_sys[j // 8][j % 8:j % 8 + 1, :], x_sys[0].shape)
            for rr in range((j + 1) // 8, hi // 8):
                x_sys[rr] = x_sys[rr] - a_sys[rr][:, j:j + 1] * row


def _solve_unit_lower(a_mats, rhs, n):
    bs = SOLVE_BLOCK
    a_blocks = [[a[r * 8:(r + 1) * 8, :] for r in range(n // 8)] for a in a_mats]
    x_blocks = [[x[r * 8:(r + 1) * 8, :] for r in range(n // 8)] for x in rhs]
    for lo in range(0, n, bs):
        if lo > 0:
            for a, x_sys in zip(a_mats, x_blocks):
                done = jnp.concatenate(x_sys[:lo // 8], axis=0)
                cur = jnp.concatenate(x_sys[lo // 8:(lo + bs) // 8], axis=0)
                cur = cur - _dot(a[lo:lo + bs, :lo], done, HI)
                x_sys[lo // 8:(lo + bs) // 8] = [cur[r * 8:(r + 1) * 8, :] for r in range(bs // 8)]
        _forward_substitute(a_blocks, x_blocks, lo, lo + bs)
    return [jnp.concatenate(x_sys, axis=0) for x_sys in x_blocks]


def _gdn_prep_kernel(qkv_ref, gt_ref, cw_ref, conv0_ref, mats_ref, qd_ref, glast_ref, hist_ref,
                     *, n_heads, valid, chunks):
    c = pl.program_id(1)
    C = GDN_CHUNK
    D = HEAD_DIM
    W = n_heads * D

    @pl.when(c == 0)
    def _():
        hist_ref[...] = conv0_ref[0]

    x = qkv_ref[0]
    rows = chunks * C
    ext = jnp.concatenate([hist_ref[...], x], axis=0)
    conv_all = ext[5:5 + rows] * cw_ref[0:1, :]
    for j in range(1, 4):
        conv_all = conv_all + ext[5 + j:5 + j + rows] * cw_ref[j:j + 1, :]
    conv_all = _silu(conv_all)
    hist_ref[...] = x[rows - 8:rows]

    ri = lax.broadcasted_iota(jnp.int32, (C, C), 0)
    ci = lax.broadcasted_iota(jnp.int32, (C, C), 1)
    tri = ci <= ri
    strict = ci < ri

    pending = []
    for n in range(chunks):
        conv = conv_all[n * C:(n + 1) * C]
        gt = gt_ref[0, n * C:(n + 1) * C, :]
        if valid < C:
            live = lax.broadcasted_iota(jnp.int32, (C, 1), 0) < valid
            conv = jnp.where(live, conv, 0.0)
            gt = jnp.where(live, gt, 0.0)
        g_cum = _dot(tri.astype(jnp.float32), gt, HI)
        g_cum_t = g_cum.T
        glast_ref[0, n] = jnp.broadcast_to(g_cum[C - 1:C, :], (8, LANES))
        a_mats, rhs_all, side = [], [], []
        for h in range(n_heads):
            q = conv[:, h * D:(h + 1) * D]
            k = conv[:, W + h * D:W + (h + 1) * D]
            v = conv[:, 2 * W + h * D:2 * W + (h + 1) * D]
            q = q * lax.rsqrt(jnp.sum(q * q, axis=-1, keepdims=True) + L2_EPS) * (D ** -0.5)
            k = k * lax.rsqrt(jnp.sum(k * k, axis=-1, keepdims=True) + L2_EPS)
            beta = gt[:, GATE_BETA + h:GATE_BETA + h + 1]
            gc = g_cum[:, GATE_G + h:GATE_G + h + 1]
            gr = g_cum_t[GATE_G + h:GATE_G + h + 1, :]
            g_last = gc[C - 1:C, :]
            decay = jnp.where(tri, jnp.exp(jnp.where(tri, gc - gr, 0.0)), 0.0)
            e_g = jnp.exp(gc)
            kb = k * beta
            a_mats.append(jnp.where(strict, _dot_nt(kb, k) * decay, 0.0))
            rhs_all.append(jnp.concatenate([v * beta, kb * e_g], axis=1))
            side.append((_dot_nt(q, k) * decay, q * e_g, (k * jnp.exp(g_last - gc)).T))
        pending.append((a_mats, rhs_all, side))

    sols = _solve_unit_lower([a for p in pending for a in p[0]], [r for p in pending for r in p[1]], C)
    for n, (_, _, side) in enumerate(pending):
        for h in range(n_heads):
            sol = sols[n * n_heads + h]
            qk, qd, kd_t = side[h]
            mats_ref[0, n, h, 0:C, :] = _dot(qk, sol)
            mats_ref[0, n, h, C:2 * C, :] = _dot(kd_t, sol)
            qd_ref[0, n, h] = qd


def _gdn_scan_kernel(mats_ref, qd_ref, glast_ref, z_ref, s0_ref, gn_ref, o_ref, s_ref,
                     *, n_heads, chunks):
    C = GDN_CHUNK
    D = HEAD_DIM

    @pl.when(pl.program_id(1) == 0)
    def _():
        s_ref[...] = s0_ref[...]

    for n in range(chunks):
        outs = []
        for h in range(n_heads):
            g_last = glast_ref[0, n][0:1, GATE_G + h:GATE_G + h + 1]
            s_prev = s_ref[0, h]
            s_low = jnp.concatenate([jnp.zeros((D, D), jnp.float32), s_prev], axis=0)
            moved = _dot(mats_ref[0, n, h], s_low)
            o = _dot(qd_ref[0, n, h], s_prev) - moved[:C] + mats_ref[0, n, h, 0:C, 0:D]
            s_ref[0, h] = s_prev * jnp.exp(g_last) - moved[C:] + mats_ref[0, n, h, C:2 * C, 0:D]
            o = o * lax.rsqrt(jnp.mean(o * o, axis=-1, keepdims=True) + RMS_EPS) * gn_ref[...]
            outs.append(o * _silu(z_ref[0, n * C:(n + 1) * C, h * D:(h + 1) * D]))
        o_ref[0, n * C:(n + 1) * C, :] = jnp.concatenate(outs, axis=1)


def _gdn(qkv, z, gt, conv_w, conv0, s0, gdn_g, valid):
    b, l, w3 = qkv.shape
    w = w3 // 3
    n_heads = w // HEAD_DIM
    C = GDN_CHUNK
    D = HEAD_DIM
    nc = l // C
    f32 = jnp.float32
    per_b = lambda shape: pl.BlockSpec((1,) + shape, lambda i, j: (i,) + (0,) * len(shape))
    chunk = lambda n, shape: pl.BlockSpec((1, n) + shape, lambda i, j: (i, j) + (0,) * len(shape))
    tok = lambda n, width: pl.BlockSpec((1, n * C, width), lambda i, j: (i, j, 0))
    cpp = max(n for n in (2, 1) if nc % n == 0)
    mats, qd, glast = pl.pallas_call(
        functools.partial(_gdn_prep_kernel, n_heads=n_heads, valid=valid, chunks=cpp),
        grid=(b, nc // cpp),
        in_specs=[tok(cpp, w3), tok(cpp, LANES), _const_spec((4, w3)), per_b((8, w3))],
        out_specs=[chunk(cpp, (n_heads, 2 * C, 2 * D)), chunk(cpp, (n_heads, C, D)), chunk(cpp, (8, LANES))],
        out_shape=[jax.ShapeDtypeStruct((b, nc, n_heads, 2 * C, 2 * D), f32),
                   jax.ShapeDtypeStruct((b, nc, n_heads, C, D), f32),
                   jax.ShapeDtypeStruct((b, nc, 8, LANES), f32)],
        scratch_shapes=[pltpu.VMEM((8, w3), f32)],
        compiler_params=_cparams(("arbitrary", "arbitrary")),
    )(qkv, gt, conv_w, conv0)
    cps = max(n for n in (8, 4, 2, 1) if nc % n == 0)
    return pl.pallas_call(
        functools.partial(_gdn_scan_kernel, n_heads=n_heads, chunks=cps),
        grid=(b, nc // cps),
        in_specs=[chunk(cps, (n_heads, 2 * C, 2 * D)), chunk(cps, (n_heads, C, D)), chunk(cps, (8, LANES)),
                  tok(cps, w), per_b((n_heads, D, D)), _const_spec((1, D))],
        out_specs=[tok(cps, w), per_b((n_heads, D, D))],
        out_shape=[jax.ShapeDtypeStruct((b, l, w), f32), jax.ShapeDtypeStruct((b, n_heads, D, D), f32)],
        compiler_params=_cparams(("arbitrary", "arbitrary")),
    )(mats, qd, glast, z, s0, gdn_g)


def _split3(x):
    hi = x.astype(jnp.bfloat16).astype(jnp.float32)
    mid = (x - hi).astype(jnp.bfloat16).astype(jnp.float32)
    return hi, mid, (x - hi) - mid


def _fox_prompt_kernel(q_ref, kt_ref, vt_ref, cum_ref, cum_t_ref, o_ref, kta_ref, vtb_ref, m_ref, l_ref,
                       acc_ref, *, tq, tk):
    hp = pl.program_id(1)
    qi = pl.program_id(2)
    bf = jnp.bfloat16
    f32 = jnp.float32
    seq = kt_ref.shape[2]
    log2e = 1.4426950408889634
    lane = lax.broadcasted_iota(jnp.int32, (tq, LANES), 1)
    low = lane < HEAD_DIM

    @pl.when(qi == 0)
    def _():
        row = lax.broadcasted_iota(jnp.int32, (LANES, tk), 0)
        sub = lax.broadcasted_iota(jnp.int32, (8, tk), 0)
        for t in range(seq // tk):
            cols = slice(t * tk, (t + 1) * tk)
            kt = kt_ref[0, :, cols]
            vtb_ref[:, cols] = vt_ref[0, :, cols].astype(bf)
            c_cols = cum_t_ref[0, :, cols]
            for hh in range(2):
                c_col = jnp.sum(jnp.where(sub == 2 * hp + hh, c_cols, 0.0), axis=0, keepdims=True)
                hi, mid, lo = _split3(-log2e * c_col)
                r = row - (HEAD_DIM if hh == 0 else 0)
                own = (row < HEAD_DIM) == (hh == 0)
                aug = jnp.where(r == 3, hi, jnp.where(r == 4, mid, jnp.where(r == 5, lo, 0.0)))
                aug = jnp.where((r >= 0) & (r < 3), 1.0, aug)
                kta_ref[hh, :, cols] = jnp.where(own, kt, aug).astype(bf)

    q = q_ref[0] * (HEAD_DIM ** -0.5 * log2e)
    cum = cum_ref[0]
    head_lane = lax.broadcasted_iota(jnp.int32, cum.shape, 1) - GATE_LOGF - 2 * hp
    q_heads = []
    for hh in range(2):
        c_row = jnp.sum(jnp.where(head_lane == hh, cum, 0.0), axis=1, keepdims=True)
        hi, mid, lo = _split3(log2e * c_row)
        r = lane - (HEAD_DIM if hh == 0 else 0)
        own = low == (hh == 0)
        aug = jnp.where(r == 0, hi, jnp.where(r == 1, mid, jnp.where(r == 2, lo, 0.0)))
        aug = jnp.where((r >= 3) & (r < 6), 1.0, aug)
        q_heads.append(jnp.where(own, q, aug).astype(bf))

    m_ref[...] = jnp.full(m_ref.shape, -jnp.inf, f32)
    l_ref[...] = jnp.zeros(l_ref.shape, f32)
    acc_ref[...] = jnp.zeros(acc_ref.shape, f32)

    def tile(r0, nr, ki, masked):
        ks = pl.multiple_of(ki * tk, tk)
        rows = slice(r0, r0 + nr)
        vt = vtb_ref[:, pl.ds(ks, tk)]
        for hh in range(2):
            s = _dot(q_heads[hh][rows], kta_ref[hh, :, pl.ds(ks, tk)])
            if masked:
                r = lax.broadcasted_iota(jnp.int32, s.shape, 0)
                cc = lax.broadcasted_iota(jnp.int32, s.shape, 1)
                s = jnp.where(cc <= r, s, -jnp.inf)
            m_prev = m_ref[hh, rows]
            m_new = jnp.maximum(m_prev, jnp.max(s, axis=1, keepdims=True))
            p = jnp.exp2(s - jnp.tile(m_new, (1, tk // LANES)))
            alpha = jnp.exp2(m_prev - m_new)
            l_ref[hh, rows] = alpha * l_ref[hh, rows] + jnp.sum(p, axis=1, keepdims=True)
            acc_ref[hh, rows] = alpha * acc_ref[hh, rows] + _dot_nt(p.astype(bf), vt)
            m_ref[hh, rows] = m_new

    def body(ki, carry):
        tile(0, tq, ki, False)
        return carry

    per = tq // tk
    lax.fori_loop(0, per * qi, body, 0)
    for a in range(per):
        for c in range(a + 1):
            tile(a * tk, tk, per * qi + c, a == c)
    o_ref[0] = jnp.where(low, acc_ref[0] / l_ref[0], acc_ref[1] / l_ref[1])


def _fox_prompt(fq, fkt, fvt, cum, cum_t, tq, tk):
    b, l, w = fq.shape
    n_pairs = w // LANES
    q_spec = pl.BlockSpec((1, tq, LANES), lambda i, hp, qi: (i, qi, hp))
    kv_spec = pl.BlockSpec((1, LANES, l), lambda i, hp, qi: (i, hp, 0))
    return pl.pallas_call(
        functools.partial(_fox_prompt_kernel, tq=tq, tk=tk),
        grid=(b, n_pairs, l // tq),
        in_specs=[q_spec, kv_spec, kv_spec,
                  pl.BlockSpec((1, tq, LANES), lambda i, hp, qi: (i, qi, 0)),
                  pl.BlockSpec((1, 8, l), lambda i, hp, qi: (i, 0, 0))],
        out_specs=q_spec,
        out_shape=jax.ShapeDtypeStruct((b, l, w), jnp.float32),
        scratch_shapes=[pltpu.VMEM((2, LANES, l), jnp.bfloat16), pltpu.VMEM((LANES, l), jnp.bfloat16),
                        pltpu.VMEM((2, tq, LANES), jnp.float32), pltpu.VMEM((2, tq, LANES), jnp.float32),
                        pltpu.VMEM((2, tq, LANES), jnp.float32)],
        compiler_params=_cparams(("arbitrary", "arbitrary", "arbitrary")),
    )(fq, fkt, fvt, cum, cum_t)


def _fox_sample_kernel(pt_ref, q_ref, kn_ref, vn_ref, gt_ref, *refs, pages_per_step, n_heads):
    del pt_ref
    pp = pages_per_step
    k_refs, v_refs, lf_refs = refs[:pp], refs[pp:2 * pp], refs[2 * pp:3 * pp]
    o_ref, m_ref, l_ref, run_ref, acc_ref = refs[3 * pp:]
    j = pl.program_id(1)
    w = n_heads * HEAD_DIM
    page = k_refs[0].shape[-1]

    sub = lax.broadcasted_iota(jnp.int32, (8, w), 0)
    own = lax.broadcasted_iota(jnp.int32, (8, w), 1) // HEAD_DIM == sub
    qb = jnp.where(own, q_ref[0] * (HEAD_DIM ** -0.5), 0.0)

    @pl.when(j == 0)
    def _():
        m_ref[...] = jnp.sum(qb * kn_ref[0], axis=1, keepdims=True)
        l_ref[...] = jnp.ones_like(l_ref)
        acc_ref[...] = jnp.broadcast_to(vn_ref[0], acc_ref.shape)
        gt = gt_ref[0]
        pick = (lax.broadcasted_iota(jnp.int32, (8, LANES), 1) - GATE_LOGF
                == lax.broadcasted_iota(jnp.int32, (8, LANES), 0))
        run_ref[...] = jnp.sum(jnp.where(pick, gt, 0.0), axis=1, keepdims=True)

    r = lax.broadcasted_iota(jnp.int32, (page, page), 0)
    c = lax.broadcasted_iota(jnp.int32, (page, page), 1)
    later = (r > c).astype(jnp.float32)
    qbb = qb.astype(jnp.bfloat16)
    run = run_ref[...]
    scores = []
    pad = jnp.zeros((8 - n_heads, page), jnp.float32)
    lf_all = jnp.concatenate([x for i in range(pp) for x in (lf_refs[i][0, 0], pad)], axis=0)
    suffix = _dot(lf_all, later, HI)
    totals = jnp.sum(lf_all, axis=1, keepdims=True)
    for i in range(pp):
        s = _dot(qbb, k_refs[i][0, 0].astype(jnp.bfloat16))
        scores.append(s + run + suffix[8 * i:8 * (i + 1)])
        run = run + totals[8 * i:8 * (i + 1)]
    run_ref[...] = run
    s_all = jnp.concatenate(scores, axis=1)
    m_prev = m_ref[...]
    m_new = jnp.maximum(m_prev, jnp.max(s_all, axis=1, keepdims=True))
    p = jnp.exp(s_all - m_new)
    alpha = jnp.exp(m_prev - m_new)
    l_ref[...] = alpha * l_ref[...] + jnp.sum(p, axis=1, keepdims=True)
    pv = _dot_nt(p[:, :page].astype(jnp.bfloat16), v_refs[0][0, 0].astype(jnp.bfloat16))
    for i in range(1, pp):
        pv = pv + _dot_nt(p[:, i * page:(i + 1) * page].astype(jnp.bfloat16),
                          v_refs[i][0, 0].astype(jnp.bfloat16))
    acc_ref[...] = alpha * acc_ref[...] + pv
    m_ref[...] = m_new

    @pl.when(j == pl.num_programs(1) - 1)
    def _():
        o = jnp.where(own, acc_ref[...] / l_ref[...], 0.0)
        o_ref[0] = jnp.sum(o, axis=0, keepdims=True)


def _fox_sample(page_table, fq, fk, fv, gt, cache_kt, cache_vt, cache_lf_t, layer, pages_per_step):
    b, _, w = fq.shape
    n_heads = w // HEAD_DIM
    page = cache_kt.shape[-1]
    n_pages = page_table.shape[1]
    pp = pages_per_step

    def page_map(i):
        return lambda bi, j, pt: (layer, pt[bi, n_pages - 1 - (j * pp + i)], 0, 0)

    tok = lambda width: pl.BlockSpec((1, 1, width), lambda bi, j, pt: (bi, 0, 0))
    in_specs = [tok(w), tok(w), tok(w), tok(LANES)]
    in_specs += [pl.BlockSpec((1, 1, w, page), page_map(i)) for i in range(pp)]
    in_specs += [pl.BlockSpec((1, 1, w, page), page_map(i)) for i in range(pp)]
    in_specs += [pl.BlockSpec((1, 1, n_heads, page), page_map(i)) for i in range(pp)]
    grid_spec = pltpu.PrefetchScalarGridSpec(
        num_scalar_prefetch=1, grid=(b, n_pages // pp), in_specs=in_specs, out_specs=tok(w),
        scratch_shapes=[pltpu.VMEM((8, 1), jnp.float32), pltpu.VMEM((8, 1), jnp.float32),
                        pltpu.VMEM((8, 1), jnp.float32), pltpu.VMEM((8, w), jnp.float32)])
    return pl.pallas_call(
        functools.partial(_fox_sample_kernel, pages_per_step=pp, n_heads=n_heads),
        grid_spec=grid_spec,
        out_shape=jax.ShapeDtypeStruct((b, 1, w), jnp.float32),
        compiler_params=_cparams(("arbitrary", "arbitrary")),
    )(page_table, fq, fk, fv, gt, *([cache_kt] * pp), *([cache_vt] * pp), *([cache_lf_t] * pp))


def _pool_kernel(u_ref, hist0_ref, w_ref, scale_ref, o_ref, hist_ref, *, pos0, tl):
    t = pl.program_id(1)

    @pl.when(t == 0)
    def _():
        hist_ref[...] = hist0_ref[0]

    u = u_ref[0]
    ext = jnp.concatenate([hist_ref[...], u], axis=0)
    hist_ref[...] = ext[tl:tl + POOL_HIST]
    sums = []
    acc = ext
    for step in (1, 2, 4, 8):
        acc = acc + pltpu.roll(acc, step, axis=0)
        sums.append(acc[POOL_HIST:])
    width = u.shape[1]
    group = lax.broadcasted_iota(jnp.int32, (tl, width), 1) // (width // len(POOL_WINDOWS))
    wsum = jnp.where(group == 0, sums[0], jnp.where(group == 1, sums[1],
                                                    jnp.where(group == 2, sums[2], sums[3])))
    window = jnp.where(group == 0, POOL_WINDOWS[0],
                       jnp.where(group == 1, POOL_WINDOWS[1],
                                 jnp.where(group == 2, POOL_WINDOWS[2], POOL_WINDOWS[3])))
    pos = pos0 + t * tl + lax.broadcasted_iota(jnp.int32, (tl, width), 0)
    cnt = jnp.minimum(pos + 1, window).astype(jnp.float32)
    d = wsum / cnt - u
    o_ref[0] = _dot(d.astype(jnp.bfloat16), w_ref[...]) * scale_ref[...]


def _pool(u, hist0, w_bd, scale, pos0, tl):
    b, l, w = u.shape
    return pl.pallas_call(
        functools.partial(_pool_kernel, pos0=pos0, tl=tl),
        grid=(b, l // tl),
        in_specs=[pl.BlockSpec((1, tl, w), lambda i, t: (i, t, 0)),
                  pl.BlockSpec((1, POOL_HIST, w), lambda i, t: (i, 0, 0)),
                  _const_spec((w, w)), _const_spec((1, w))],
        out_specs=pl.BlockSpec((1, tl, w), lambda i, t: (i, t, 0)),
        out_shape=jax.ShapeDtypeStruct((b, l, w), jnp.float32),
        scratch_shapes=[pltpu.VMEM((POOL_HIST, w), jnp.float32)],
        compiler_params=_cparams(("arbitrary", "arbitrary")),
    )(u, hist0, w_bd, scale)


def _tail_kernel(x_ref, oa_ref, of_ref, oc_ref, wo_ref, gf_ref, wg_ref, wu_ref, wd_ref, gl_ref, o_ref,
                 *, ff_chunks, final_norm):
    bf = jnp.bfloat16
    mixed = jnp.concatenate([oa_ref[...], of_ref[...], oc_ref[...]], axis=1).astype(bf)
    x = x_ref[...] + _dot(mixed, wo_ref[...])
    h = (x * lax.rsqrt(jnp.mean(x * x, axis=-1, keepdims=True) + RMS_EPS) * gf_ref[...]).astype(bf)
    for c0, c1 in ff_chunks:
        act = _silu(_dot(h, wg_ref[:, c0:c1])) * _dot(h, wu_ref[:, c0:c1])
        x = x + _dot(act.astype(bf), wd_ref[c0:c1, :])
    if final_norm:
        x = x * lax.rsqrt(jnp.mean(x * x, axis=-1, keepdims=True) + RMS_EPS) * gl_ref[...]
    o_ref[...] = x


def _tail(x2d, oa, of, oc, wo, g_ffn, wg, wu, wd, g_last, final_norm, tm):
    n, d = x2d.shape
    d_ff = wg.shape[1]
    step = 1024
    ff_chunks = tuple((c0, min(c0 + step, d_ff)) for c0 in range(0, d_ff, step))
    row_spec = lambda w: pl.BlockSpec((tm, w), lambda i: (i, 0))
    weights = (wo, g_ffn, wg, wu, wd, g_last)
    return pl.pallas_call(
        functools.partial(_tail_kernel, ff_chunks=ff_chunks, final_norm=final_norm),
        grid=(n // tm,),
        in_specs=[row_spec(d), row_spec(oa.shape[1]), row_spec(of.shape[1]), row_spec(oc.shape[1])]
        + [_const_spec(a.shape) for a in weights],
        out_specs=row_spec(d),
        out_shape=jax.ShapeDtypeStruct((n, d), jnp.float32),
        compiler_params=_cparams(("arbitrary",)),
    )(x2d, oa, of, oc, *weights)


def _layer_weights(l, w_gdn, w_fox, w_pool, norm_mix_g, w_in_t, conv_w, a_log, dt_bias, gdn_norm_g,
                   fox_bf, pool_w, pool_scale, w_out, norm_ffn_g, w_gate_up, w_down):
    bf = jnp.bfloat16
    n_hg = w_gdn // HEAD_DIM
    n_hf = w_fox // HEAD_DIM
    d_ff = w_down.shape[1]
    o = 0
    src = {}
    for name, width in (("qkv", 3 * w_gdn), ("z", w_gdn), ("beta", n_hg), ("alpha", n_hg),
                        ("fq", w_fox), ("fk", w_fox), ("fv", w_fox), ("fgate", n_hf), ("pu", w_pool)):
        src[name] = (o, o + width)
        o += width
    wl = w_in_t[:, l, :]
    rows = lambda name: wl[src[name][0]:src[name][1]]
    gate_rows = jnp.concatenate([rows("beta"), rows("alpha"), rows("fgate")], axis=0)
    gate_rows = jnp.pad(gate_rows, ((0, LANES - gate_rows.shape[0]), (0, 0)))
    order = ("qkv", "z", "fq", "fk", "fv", "pu")
    w_t = jnp.concatenate([rows(n) for n in order] + [gate_rows], axis=0).astype(bf)
    seg, o = {}, 0
    for name in order:
        width = src[name][1] - src[name][0]
        seg[name] = (o, o + width)
        o += width
    seg["gates"] = (o, o + LANES)
    gp = jnp.zeros((8, LANES), jnp.float32)
    gp = gp.at[0, GATE_G:GATE_G + n_hg].set(a_log[l])
    gp = gp.at[1, GATE_G:GATE_G + n_hg].set(dt_bias[l])
    gp = gp.at[1, GATE_LOGF:GATE_LOGF + n_hf].set(fox_bf[l])
    n_groups, pg, _ = pool_w[l].shape
    w_bd = jnp.zeros((w_pool, w_pool), jnp.float32)
    for gi in range(n_groups):
        w_bd = w_bd.at[gi * pg:(gi + 1) * pg, gi * pg:(gi + 1) * pg].set(pool_w[l, gi])
    return dict(
        g_mix=norm_mix_g[l][None, :], w_t=w_t, seg=seg, gp=gp, conv_w=conv_w[l],
        gdn_g=gdn_norm_g[l][None, :], w_bd=w_bd.astype(bf), pool_scale=pool_scale[l][None, :],
        wo=w_out[l].astype(bf),
        g_ffn=norm_ffn_g[l][None, :], wg=w_gate_up[l][:, :d_ff].astype(bf),
        wu=w_gate_up[l][:, d_ff:].astype(bf), wd=w_down[l].astype(bf))


def kernel(x_prompt, x_sample, cache_k, cache_v, cache_logf, state_gdn, state_conv, state_pool,
           page_table, norm_mix_g, w_in, conv_w, a_log, dt_bias, gdn_norm_g, fox_bf, pool_w,
           pool_scale, w_out, norm_ffn_g, w_gate_up, w_down, final_norm_g):
    f32 = jnp.float32
    bp, seq, d = x_prompt.shape
    bs, dec_seq, _ = x_sample.shape
    depth, n_pool, page, n_hf, _ = cache_k.shape
    n_hg = state_gdn.shape[2]
    w_gdn = n_hg * HEAD_DIM
    w_fox = n_hf * HEAD_DIM
    w_pool = state_pool.shape[-1]
    pool_buf = state_pool.shape[2]
    conv_hist = state_conv.shape[2]
    past_len = page_table.shape[1] * page
    C = GDN_CHUNK
    assert dec_seq == 1 and seq % LANES == 0 and conv_hist == 3 and pool_buf == POOL_HIST - 1
    assert bs <= LANES

    tm_p = min(512, seq)
    tk = min(512, seq)
    tq = 2 * tk if seq % (2 * tk) == 0 else tk
    tl = min(512, seq)
    pps = max(p for p in (32, 16, 8, 4, 2, 1) if page_table.shape[1] % p == 0)
    g_last = final_norm_g[None, :]

    xp = x_prompt.reshape(bp * seq, d)
    xs = x_sample.reshape(bs, d)
    cache_kt = jnp.transpose(cache_k, (0, 1, 3, 4, 2)).reshape(depth, n_pool, w_fox, page)
    cache_vt = jnp.transpose(cache_v, (0, 1, 3, 4, 2)).reshape(depth, n_pool, w_fox, page)
    cache_lf_t = jnp.swapaxes(cache_logf, 2, 3)
    w_in_t = jnp.transpose(w_in, (2, 0, 1))

    st_p, st_s = [], []
    for l in range(depth):
        lw = _layer_weights(l, w_gdn, w_fox, w_pool, norm_mix_g, w_in_t, conv_w, a_log, dt_bias,
                            gdn_norm_g, fox_bf, pool_w, pool_scale, w_out, norm_ffn_g, w_gate_up, w_down)
        final = l == depth - 1

        qkv, z, fq, fkt, fvt, pu, gt, cum, lft, cumt = _in_proj(
            xp.reshape(bp, seq, d), lw["g_mix"], lw["w_t"], lw["gp"], lw["seg"], tm_p)
        o_a, s_fin = _gdn(qkv, z, gt, lw["conv_w"], jnp.zeros((bp, 8, 3 * w_gdn), f32),
                          jnp.zeros((bp, n_hg, HEAD_DIM, HEAD_DIM), f32), lw["gdn_g"], C)
        o_f = _fox_prompt(fq, fkt, fvt, cum, cumt, tq, tk)
        o_c = _pool(pu, jnp.zeros((bp, POOL_HIST, w_pool), f32), lw["w_bd"], lw["pool_scale"], 0, tl)
        xp = _tail(xp, o_a.reshape(bp * seq, w_gdn), o_f.reshape(bp * seq, w_fox),
                   o_c.reshape(bp * seq, w_pool), lw["wo"], lw["g_ffn"],
                   lw["wg"], lw["wu"], lw["wd"], g_last, final, tm_p)
        heads_last = lambda a: jnp.transpose(a.reshape(bp, n_hf, HEAD_DIM, seq), (0, 3, 1, 2))
        st_p.append((heads_last(fkt), heads_last(fvt), jnp.swapaxes(lft[:, :n_hf, :], 1, 2), s_fin,
                     qkv[:, seq - conv_hist:], pu[:, seq - pool_buf:]))

        xs_rows = jnp.pad(xs, ((0, LANES - bs), (0, 0)))[None]
        qkv, z, fq, fkt, fvt, pu, gt, _, _, _ = _in_proj(xs_rows, lw["g_mix"], lw["w_t"], lw["gp"],
                                                         lw["seg"], LANES)
        qkv, z, fq, pu, gt = (a[0, :bs] for a in (qkv, z, fq, pu, gt))
        fk, fv = (jnp.swapaxes(a[0, :, :bs], 0, 1) for a in (fkt, fvt))
        pad_c = lambda a: jnp.pad(a[:, None, :], ((0, 0), (0, C - 1), (0, 0)))
        conv0 = jnp.pad(state_conv[l], ((0, 0), (8 - conv_hist, 0), (0, 0)))
        o_a, s_fin = _gdn(pad_c(qkv), pad_c(z), pad_c(gt), lw["conv_w"], conv0, state_gdn[l],
                          lw["gdn_g"], 1)
        o_f = _fox_sample(page_table, fq[:, None, :], fk[:, None, :], fv[:, None, :], gt[:, None, :],
                          cache_kt, cache_vt, cache_lf_t, l, pps)
        hist0 = jnp.pad(state_pool[l], ((0, 0), (POOL_HIST - pool_buf, 0), (0, 0)))
        o_c = _pool(jnp.pad(pu[:, None, :], ((0, 0), (0, 7), (0, 0))), hist0, lw["w_bd"],
                    lw["pool_scale"], past_len, 8)
        xs = _tail(xs, o_a[:, 0], o_f[:, 0], o_c[:, 0], lw["wo"], lw["g_ffn"],
                   lw["wg"], lw["wu"], lw["wd"], g_last, final, bs)
        st_s.append((fk.reshape(bs, 1, n_hf, HEAD_DIM), fv.reshape(bs, 1, n_hf, HEAD_DIM),
                     gt[:, None, GATE_LOGF:GATE_LOGF + n_hf], s_fin,
                     jnp.concatenate([state_conv[l][:, 1:], qkv[:, None, :]], axis=1),
                     jnp.concatenate([state_pool[l][:, 1:], pu[:, None, :]], axis=1)))

    outs = [xp.reshape(bp, seq, d), xs.reshape(bs, 1, d)]
    for st in (st_p, st_s):
        for i in range(6):
            outs.append(jnp.stack([s[i] for s in st]))
    return tuple(outs)
```

```python
import functools

import jax
import jax.numpy as jnp
from jax import lax
from jax.experimental import pallas as pl
from jax.experimental.pallas import tpu as pltpu

HEAD_DIM = 64
POOL_WINDOWS = (2, 4, 8, 16)
POOL_HIST = 16
GDN_CHUNK = 64
SOLVE_BLOCK = 8
RMS_EPS = 1e-6
L2_EPS = 1e-6
LANES = 128
VMEM_LIMIT = 56 * 1024 * 1024
HI = lax.Precision.HIGHEST

GATE_BETA = 0
GATE_G = 6
GATE_LOGF = 12


def _cparams(sem):
    return pltpu.CompilerParams(dimension_semantics=sem, vmem_limit_bytes=VMEM_LIMIT)


def _const_spec(shape):
    nd = len(shape)
    return pl.BlockSpec(shape, lambda *_: (0,) * nd, pipeline_mode=pl.Buffered(1))


def _softplus(x):
    return jnp.maximum(x, 0.0) + jnp.log1p(jnp.exp(-jnp.abs(x)))


def _silu(x):
    return x * jax.nn.sigmoid(x)


def _dot_nt(a, b, precision=None):
    return lax.dot_general(a, b, (((1,), (1,)), ((), ())), precision=precision,
                           preferred_element_type=jnp.float32)


def _dot(a, b, precision=None):
    return jnp.dot(a, b, precision=precision, preferred_element_type=jnp.float32)


def _in_proj_kernel(x_ref, g_ref, wt_ref, gp_ref, qkv_ref, z_ref, fq_ref, fkt_ref, fvt_ref, pu_ref,
                    gt_ref, cum_ref, lft_ref, cumt_ref, carry_ref, *, seg, tm):
    x = x_ref[0]
    h = x * lax.rsqrt(jnp.mean(x * x, axis=-1, keepdims=True) + RMS_EPS) * g_ref[...]
    hb = h.astype(jnp.bfloat16)
    for name, o_ref in (("qkv", qkv_ref), ("z", z_ref), ("fq", fq_ref), ("pu", pu_ref)):
        r0, r1 = seg[name]
        o_ref[0] = _dot_nt(hb, wt_ref[r0:r1, :])
    for name, o_ref in (("fk", fkt_ref), ("fv", fvt_ref)):
        r0, r1 = seg[name]
        o_ref[0] = _dot_nt(wt_ref[r0:r1, :], hb)
    r0, r1 = seg["gates"]
    raw = _dot_nt(hb, wt_ref[r0:r1, :])
    lane = lax.broadcasted_iota(jnp.int32, raw.shape, 1)
    a_log = gp_ref[0:1, :]
    shifted = raw + gp_ref[1:2, :]
    beta = jax.nn.sigmoid(raw)
    g = -jnp.exp(a_log) * _softplus(shifted)
    logf = -_softplus(-shifted)
    gt = jnp.where(lane < GATE_G, beta,
                   jnp.where(lane < GATE_LOGF, g,
                             jnp.where(lane < GATE_LOGF + 6, logf, 0.0)))
    gt_ref[0] = gt

    @pl.when(pl.program_id(1) == 0)
    def _():
        carry_ref[...] = jnp.zeros_like(carry_ref)

    r = lax.broadcasted_iota(jnp.int32, (LANES, LANES), 0)
    c = lax.broadcasted_iota(jnp.int32, (LANES, LANES), 1)
    tril = (c <= r).astype(jnp.float32)
    carry = carry_ref[...]
    for s in range(tm // LANES):
        rows = slice(s * LANES, (s + 1) * LANES)
        cs = _dot(tril, gt[rows], HI) + carry
        cum_ref[0, rows, :] = cs
        cumt_ref[0, :, rows] = cs.T[GATE_LOGF:GATE_LOGF + 8, :]
        lft_ref[0, :, rows] = gt[rows].T[GATE_LOGF:GATE_LOGF + 8, :]
        carry = cs[LANES - 1:LANES, :]
    carry_ref[...] = carry


def _in_proj(x, g_mix, w_t, gate_params, seg, tm):
    b, l, d = x.shape
    row = lambda w: (jax.ShapeDtypeStruct((b, l, w), jnp.float32),
                     pl.BlockSpec((1, tm, w), lambda i, t: (i, t, 0)))
    col = lambda w: (jax.ShapeDtypeStruct((b, w, l), jnp.float32),
                     pl.BlockSpec((1, w, tm), lambda i, t: (i, 0, t)))
    width = lambda name: seg[name][1] - seg[name][0]
    outs = [row(width("qkv")), row(width("z")), row(width("fq")), col(width("fk")), col(width("fv")),
            row(width("pu")), row(LANES), row(LANES), col(8), col(8)]
    return pl.pallas_call(
        functools.partial(_in_proj_kernel, seg=seg, tm=tm),
        grid=(b, l // tm),
        in_specs=[pl.BlockSpec((1, tm, d), lambda i, t: (i, t, 0)), _const_spec((1, d)),
                  _const_spec(w_t.shape), _const_spec((8, LANES))],
        out_specs=[o[1] for o in outs],
        out_shape=[o[0] for o in outs],
        scratch_shapes=[pltpu.VMEM((1, LANES), jnp.float32)],
        compiler_params=_cparams(("arbitrary", "arbitrary")),
    )(x, g_mix, w_t, gate_params)


def _forward_substitute(a_blocks, x_blocks, lo, hi):
    for j in range(lo, hi - 1):
        for a_sys, x_sys in zip(a_blocks, x_blocks):
            row = jnp.broadcast_to(x_sys[j // 8][j % 8:j % 8 + 1, :], x_sys[0].shape)
            for rr in range((j + 1) // 8, hi // 8):
                x_sys[rr] = x_sys[rr] - a_sys[rr][:, j:j + 1] * row


def _solve_unit_lower(a_mats, rhs, n):
    bs = SOLVE_BLOCK
    a_blocks = [[a[r * 8:(r + 1) * 8, :] for r in range(n // 8)] for a in a_mats]
    x_blocks = [[x[r * 8:(r + 1) * 8, :] for r in range(n // 8)] for x in rhs]
    for lo in range(0, n, bs):
        if lo > 0:
            for a, x_sys in zip(a_mats, x_blocks):
                done = jnp.concatenate(x_sys[:lo // 8], axis=0)
                cur = jnp.concatenate(x_sys[lo // 8:(lo + bs) // 8], axis=0)
                cur = cur - _dot(a[lo:lo + bs, :lo], done, HI)
                x_sys[lo // 8:(lo + bs) // 8] = [cur[r * 8:(r + 1) * 8, :] for r in range(bs // 8)]
        _forward_substitute(a_blocks, x_blocks, lo, lo + bs)
    return [jnp.concatenate(x_sys, axis=0) for x_sys in x_blocks]


def _gdn_prep_kernel(qkv_ref, gt_ref, cw_ref, conv0_ref, mats_ref, qd_ref, glast_ref, hist_ref,
                     *, n_heads, valid, chunks):
    c = pl.program_id(1)
    C = GDN_CHUNK
    D = HEAD_DIM
    W = n_heads * D

    @pl.when(c == 0)
    def _():
        hist_ref[...] = conv0_ref[0]

    x = qkv_ref[0]
    rows = chunks * C
    ext = jnp.concatenate([hist_ref[...], x], axis=0)
    conv_all = ext[5:5 + rows] * cw_ref[0:1, :]
    for j in range(1, 4):
        conv_all = conv_all + ext[5 + j:5 + j + rows] * cw_ref[j:j + 1, :]
    conv_all = _silu(conv_all)
    hist_ref[...] = x[rows - 8:rows]

    ri = lax.broadcasted_iota(jnp.int32, (C, C), 0)
    ci = lax.broadcasted_iota(jnp.int32, (C, C), 1)
    tri = ci <= ri
    strict = ci < ri

    pending = []
    for n in range(chunks):
        conv = conv_all[n * C:(n + 1) * C]
        gt = gt_ref[0, n * C:(n + 1) * C, :]
        if valid < C:
            live = lax.broadcasted_iota(jnp.int32, (C, 1), 0) < valid
            conv = jnp.where(live, conv, 0.0)
            gt = jnp.where(live, gt, 0.0)
        g_cum = _dot(tri.astype(jnp.float32), gt, HI)
        g_cum_t = g_cum.T
        glast_ref[0, n] = jnp.broadcast_to(g_cum[C - 1:C, :], (8, LANES))
        a_mats, rhs_all, side = [], [], []
        for h in range(n_heads):
            q = conv[:, h * D:(h + 1) * D]
            k = conv[:, W + h * D:W + (h + 1) * D]
            v = conv[:, 2 * W + h * D:2 * W + (h + 1) * D]
            q = q * lax.rsqrt(jnp.sum(q * q, axis=-1, keepdims=True) + L2_EPS) * (D ** -0.5)
            k = k * lax.rsqrt(jnp.sum(k * k, axis=-1, keepdims=True) + L2_EPS)
            beta = gt[:, GATE_BETA + h:GATE_BETA + h + 1]
            gc = g_cum[:, GATE_G + h:GATE_G + h + 1]
            gr = g_cum_t[GATE_G + h:GATE_G + h + 1, :]
            g_last = gc[C - 1:C, :]
            decay = jnp.where(tri, jnp.exp(jnp.where(tri, gc - gr, 0.0)), 0.0)
            e_g = jnp.exp(gc)
            kb = k * beta
            a_mats.append(jnp.where(strict, _dot_nt(kb, k) * decay, 0.0))
            rhs_all.append(jnp.concatenate([v * beta, kb * e_g], axis=1))
            side.append((_dot_nt(q, k) * decay, q * e_g, (k * jnp.exp(g_last - gc)).T))
        pending.append((a_mats, rhs_all, side))

    sols = _solve_unit_lower([a for p in pending for a in p[0]], [r for p in pending for r in p[1]], C)
    for n, (_, _, side) in enumerate(pending):
        for h in range(n_heads):
            sol = sols[n * n_heads + h]
            qk, qd, kd_t = side[h]
            mats_ref[0, n, h, 0:C, :] = _dot(qk, sol)
            mats_ref[0, n, h, C:2 * C, :] = _dot(kd_t, sol)
            qd_ref[0, n, h] = qd


def _gdn_scan_kernel(mats_ref, qd_ref, glast_ref, z_ref, s0_ref, gn_ref, o_ref, s_ref,
                     *, n_heads, chunks):
    C = GDN_CHUNK
    D = HEAD_DIM

    @pl.when(pl.program_id(1) == 0)
    def _():
        s_ref[...] = s0_ref[...]

    for n in range(chunks):
        outs = []
        for h in range(n_heads):
            g_last = glast_ref[0, n][0:1, GATE_G + h:GATE_G + h + 1]
            s_prev = s_ref[0, h]
            s_low = jnp.concatenate([jnp.zeros((D, D), jnp.float32), s_prev], axis=0)
            moved = _dot(mats_ref[0, n, h], s_low)
            o = _dot(qd_ref[0, n, h], s_prev) - moved[:C] + mats_ref[0, n, h, 0:C, 0:D]
            s_ref[0, h] = s_prev * jnp.exp(g_last) - moved[C:] + mats_ref[0, n, h, C:2 * C, 0:D]
            o = o * lax.rsqrt(jnp.mean(o * o, axis=-1, keepdims=True) + RMS_EPS) * gn_ref[...]
            outs.append(o * _silu(z_ref[0, n * C:(n + 1) * C, h * D:(h + 1) * D]))
        o_ref[0, n * C:(n + 1) * C, :] = jnp.concatenate(outs, axis=1)


def _gdn(qkv, z, gt, conv_w, conv0, s0, gdn_g, valid):
    b, l, w3 = qkv.shape
    w = w3 // 3
    n_heads = w // HEAD_DIM
    C = GDN_CHUNK
    D = HEAD_DIM
    nc = l // C
    f32 = jnp.float32
    per_b = lambda shape: pl.BlockSpec((1,) + shape, lambda i, j: (i,) + (0,) * len(shape))
    chunk = lambda n, shape: pl.BlockSpec((1, n) + shape, lambda i, j: (i, j) + (0,) * len(shape))
    tok = lambda n, width: pl.BlockSpec((1, n * C, width), lambda i, j: (i, j, 0))
    cpp = max(n for n in (2, 1) if nc % n == 0)
    mats, qd, glast = pl.pallas_call(
        functools.partial(_gdn_prep_kernel, n_heads=n_heads, valid=valid, chunks=cpp),
        grid=(b, nc // cpp),
        in_specs=[tok(cpp, w3), tok(cpp, LANES), _const_spec((4, w3)), per_b((8, w3))],
        out_specs=[chunk(cpp, (n_heads, 2 * C, 2 * D)), chunk(cpp, (n_heads, C, D)), chunk(cpp, (8, LANES))],
        out_shape=[jax.ShapeDtypeStruct((b, nc, n_heads, 2 * C, 2 * D), f32),
                   jax.ShapeDtypeStruct((b, nc, n_heads, C, D), f32),
                   jax.ShapeDtypeStruct((b, nc, 8, LANES), f32)],
        scratch_shapes=[pltpu.VMEM((8, w3), f32)],
        compiler_params=_cparams(("arbitrary", "arbitrary")),
    )(qkv, gt, conv_w, conv0)
    cps = max(n for n in (8, 4, 2, 1) if nc % n == 0)
    return pl.pallas_call(
        functools.partial(_gdn_scan_kernel, n_heads=n_heads, chunks=cps),
        grid=(b, nc // cps),
        in_specs=[chunk(cps, (n_heads, 2 * C, 2 * D)), chunk(cps, (n_heads, C, D)), chunk(cps, (8, LANES)),
                  tok(cps, w), per_b((n_heads, D, D)), _const_spec((1, D))],
        out_specs=[tok(cps, w), per_b((n_heads, D, D))],
        out_shape=[jax.ShapeDtypeStruct((b, l, w), f32), jax.ShapeDtypeStruct((b, n_heads, D, D), f32)],
        compiler_params=_cparams(("arbitrary", "arbitrary")),
    )(mats, qd, glast, z, s0, gdn_g)


def _split3(x):
    hi = x.astype(jnp.bfloat16).astype(jnp.float32)
    mid = (x - hi).astype(jnp.bfloat16).astype(jnp.float32)
    return hi, mid, (x - hi) - mid


def _fox_prompt_kernel(q_ref, kt_ref, vt_ref, cum_ref, cum_t_ref, o_ref, kta_ref, vtb_ref, m_ref, l_ref,
                       acc_ref, *, tq, tk):
    hp = pl.program_id(1)
    qi = pl.program_id(2)
    bf = jnp.bfloat16
    f32 = jnp.float32
    seq = kt_ref.shape[2]
    log2e = 1.4426950408889634
    lane = lax.broadcasted_iota(jnp.int32, (tq, LANES), 1)
    low = lane < HEAD_DIM

    @pl.when(qi == 0)
    def _():
        row = lax.broadcasted_iota(jnp.int32, (LANES, tk), 0)
        sub = lax.broadcasted_iota(jnp.int32, (8, tk), 0)
        for t in range(seq // tk):
            cols = slice(t * tk, (t + 1) * tk)
            kt = kt_ref[0, :, cols]
            vtb_ref[:, cols] = vt_ref[0, :, cols].astype(bf)
            c_cols = cum_t_ref[0, :, cols]
            for hh in range(2):
                c_col = jnp.sum(jnp.where(sub == 2 * hp + hh, c_cols, 0.0), axis=0, keepdims=True)
                hi, mid, lo = _split3(-log2e * c_col)
                r = row - (HEAD_DIM if hh == 0 else 0)
                own = (row < HEAD_DIM) == (hh == 0)
                aug = jnp.where(r == 3, hi, jnp.where(r == 4, mid, jnp.where(r == 5, lo, 0.0)))
                aug = jnp.where((r >= 0) & (r < 3), 1.0, aug)
                kta_ref[hh, :, cols] = jnp.where(own, kt, aug).astype(bf)

    q = q_ref[0] * (HEAD_DIM ** -0.5 * log2e)
    cum = cum_ref[0]
    head_lane = lax.broadcasted_iota(jnp.int32, cum.shape, 1) - GATE_LOGF - 2 * hp
    q_heads = []
    for hh in range(2):
        c_row = jnp.sum(jnp.where(head_lane == hh, cum, 0.0), axis=1, keepdims=True)
        hi, mid, lo = _split3(log2e * c_row)
        r = lane - (HEAD_DIM if hh == 0 else 0)
        own = low == (hh == 0)
        aug = jnp.where(r == 0, hi, jnp.where(r == 1, mid, jnp.where(r == 2, lo, 0.0)))
        aug = jnp.where((r >= 3) & (r < 6), 1.0, aug)
        q_heads.append(jnp.where(own, q, aug).astype(bf))

    m_ref[...] = jnp.full(m_ref.shape, -jnp.inf, f32)
    l_ref[...] = jnp.zeros(l_ref.shape, f32)
    acc_ref[...] = jnp.zeros(acc_ref.shape, f32)

    def tile(r0, nr, ki, masked):
        ks = pl.multiple_of(ki * tk, tk)
        rows = slice(r0, r0 + nr)
        vt = vtb_ref[:, pl.ds(ks, tk)]
        for hh in range(2):
            s = _dot(q_heads[hh][rows], kta_ref[hh, :, pl.ds(ks, tk)])
            if masked:
                r = lax.broadcasted_iota(jnp.int32, s.shape, 0)
                cc = lax.broadcasted_iota(jnp.int32, s.shape, 1)
                s = jnp.where(cc <= r, s, -jnp.inf)
            m_prev = m_ref[hh, rows]
            m_new = jnp.maximum(m_prev, jnp.max(s, axis=1, keepdims=True))
            p = jnp.exp2(s - jnp.tile(m_new, (1, tk // LANES)))
            alpha = jnp.exp2(m_prev - m_new)
            l_ref[hh, rows] = alpha * l_ref[hh, rows] + jnp.sum(p, axis=1, keepdims=True)
            acc_ref[hh, rows] = alpha * acc_ref[hh, rows] + _dot_nt(p.astype(bf), vt)
            m_ref[hh, rows] = m_new

    def body(ki, carry):
        tile(0, tq, ki, False)
        return carry

    per = tq // tk
    lax.fori_loop(0, per * qi, body, 0)
    for a in range(per):
        for c in range(a + 1):
            tile(a * tk, tk, per * qi + c, a == c)
    o_ref[0] = jnp.where(low, acc_ref[0] / l_ref[0], acc_ref[1] / l_ref[1])


def _fox_prompt(fq, fkt, fvt, cum, cum_t, tq, tk):
    b, l, w = fq.shape
    n_pairs = w // LANES
    q_spec = pl.BlockSpec((1, tq, LANES), lambda i, hp, qi: (i, qi, hp))
    kv_spec = pl.BlockSpec((1, LANES, l), lambda i, hp, qi: (i, hp, 0))
    return pl.pallas_call(
        functools.partial(_fox_prompt_kernel, tq=tq, tk=tk),
        grid=(b, n_pairs, l // tq),
        in_specs=[q_spec, kv_spec, kv_spec,
                  pl.BlockSpec((1, tq, LANES), lambda i, hp, qi: (i, qi, 0)),
                  pl.BlockSpec((1, 8, l), lambda i, hp, qi: (i, 0, 0))],
        out_specs=q_spec,
        out_shape=jax.ShapeDtypeStruct((b, l, w), jnp.float32),
        scratch_shapes=[pltpu.VMEM((2, LANES, l), jnp.bfloat16), pltpu.VMEM((LANES, l), jnp.bfloat16),
                        pltpu.VMEM((2, tq, LANES), jnp.float32), pltpu.VMEM((2, tq, LANES), jnp.float32),
                        pltpu.VMEM((2, tq, LANES), jnp.float32)],
        compiler_params=_cparams(("arbitrary", "arbitrary", "arbitrary")),
    )(fq, fkt, fvt, cum, cum_t)


def _fox_sample_kernel(pt_ref, q_ref, kn_ref, vn_ref, gt_ref, *refs, pages_per_step, n_heads):
    del pt_ref
    pp = pages_per_step
    k_refs, v_refs, lf_refs = refs[:pp], refs[pp:2 * pp], refs[2 * pp:3 * pp]
    o_ref, m_ref, l_ref, run_ref, acc_ref = refs[3 * pp:]
    j = pl.program_id(1)
    w = n_heads * HEAD_DIM
    page = k_refs[0].shape[-1]

    sub = lax.broadcasted_iota(jnp.int32, (8, w), 0)
    own = lax.broadcasted_iota(jnp.int32, (8, w), 1) // HEAD_DIM == sub
    qb = jnp.where(own, q_ref[0] * (HEAD_DIM ** -0.5), 0.0)

    @pl.when(j == 0)
    def _():
        m_ref[...] = jnp.sum(qb * kn_ref[0], axis=1, keepdims=True)
        l_ref[...] = jnp.ones_like(l_ref)
        acc_ref[...] = jnp.broadcast_to(vn_ref[0], acc_ref.shape)
        gt = gt_ref[0]
        pick = (lax.broadcasted_iota(jnp.int32, (8, LANES), 1) - GATE_LOGF
                == lax.broadcasted_iota(jnp.int32, (8, LANES), 0))
        run_ref[...] = jnp.sum(jnp.where(pick, gt, 0.0), axis=1, keepdims=True)

    r = lax.broadcasted_iota(jnp.int32, (page, page), 0)
    c = lax.broadcasted_iota(jnp.int32, (page, page), 1)
    later = (r > c).astype(jnp.float32)
    qbb = qb.astype(jnp.bfloat16)
    run = run_ref[...]
    scores = []
    pad = jnp.zeros((8 - n_heads, page), jnp.float32)
    lf_all = jnp.concatenate([x for i in range(pp) for x in (lf_refs[i][0, 0], pad)], axis=0)
    suffix = _dot(lf_all, later, HI)
    totals = jnp.sum(lf_all, axis=1, keepdims=True)
    for i in range(pp):
        s = _dot(qbb, k_refs[i][0, 0].astype(jnp.bfloat16))
        scores.append(s + run + suffix[8 * i:8 * (i + 1)])
        run = run + totals[8 * i:8 * (i + 1)]
    run_ref[...] = run
    s_all = jnp.concatenate(scores, axis=1)
    m_prev = m_ref[...]
    m_new = jnp.maximum(m_prev, jnp.max(s_all, axis=1, keepdims=True))
    p = jnp.exp(s_all - m_new)
    alpha = jnp.exp(m_prev - m_new)
    l_ref[...] = alpha * l_ref[...] + jnp.sum(p, axis=1, keepdims=True)
    pv = _dot_nt(p[:, :page].astype(jnp.bfloat16), v_refs[0][0, 0].astype(jnp.bfloat16))
    for i in range(1, pp):
        pv = pv + _dot_nt(p[:, i * page:(i + 1) * page].astype(jnp.bfloat16),
                          v_refs[i][0, 0].astype(jnp.bfloat16))
    acc_ref[...] = alpha * acc_ref[...] + pv
    m_ref[...] = m_new

    @pl.when(j == pl.num_programs(1) - 1)
    def _():
        o = jnp.where(own, acc_ref[...] / l_ref[...], 0.0)
        o_ref[0] = jnp.sum(o, axis=0, keepdims=True)


def _fox_sample(page_table, fq, fk, fv, gt, cache_kt, cache_vt, cache_lf_t, layer, pages_per_step):
    b, _, w = fq.shape
    n_heads = w // HEAD_DIM
    page = cache_kt.shape[-1]
    n_pages = page_table.shape[1]
    pp = pages_per_step

    def page_map(i):
        return lambda bi, j, pt: (layer, pt[bi, n_pages - 1 - (j * pp + i)], 0, 0)

    tok = lambda width: pl.BlockSpec((1, 1, width), lambda bi, j, pt: (bi, 0, 0))
    in_specs = [tok(w), tok(w), tok(w), tok(LANES)]
    in_specs += [pl.BlockSpec((1, 1, w, page), page_map(i)) for i in range(pp)]
    in_specs += [pl.BlockSpec((1, 1, w, page), page_map(i)) for i in range(pp)]
    in_specs += [pl.BlockSpec((1, 1, n_heads, page), page_map(i)) for i in range(pp)]
    grid_spec = pltpu.PrefetchScalarGridSpec(
        num_scalar_prefetch=1, grid=(b, n_pages // pp), in_specs=in_specs, out_specs=tok(w),
        scratch_shapes=[pltpu.VMEM((8, 1), jnp.float32), pltpu.VMEM((8, 1), jnp.float32),
                        pltpu.VMEM((8, 1), jnp.float32), pltpu.VMEM((8, w), jnp.float32)])
    return pl.pallas_call(
        functools.partial(_fox_sample_kernel, pages_per_step=pp, n_heads=n_heads),
        grid_spec=grid_spec,
        out_shape=jax.ShapeDtypeStruct((b, 1, w), jnp.float32),
        compiler_params=_cparams(("arbitrary", "arbitrary")),
    )(page_table, fq, fk, fv, gt, *([cache_kt] * pp), *([cache_vt] * pp), *([cache_lf_t] * pp))


def _pool_kernel(u_ref, hist0_ref, w_ref, scale_ref, o_ref, hist_ref, *, pos0, tl):
    t = pl.program_id(1)

    @pl.when(t == 0)
    def _():
        hist_ref[...] = hist0_ref[0]

    u = u_ref[0]
    ext = jnp.concatenate([hist_ref[...], u], axis=0)
    hist_ref[...] = ext[tl:tl + POOL_HIST]
    sums = []
    acc = ext
    for step in (1, 2, 4, 8):
        acc = acc + pltpu.roll(acc, step, axis=0)
        sums.append(acc[POOL_HIST:])
    width = u.shape[1]
    group = lax.broadcasted_iota(jnp.int32, (tl, width), 1) // (width // len(POOL_WINDOWS))
    wsum = jnp.where(group == 0, sums[0], jnp.where(group == 1, sums[1],
                                                    jnp.where(group == 2, sums[2], sums[3])))
    window = jnp.where(group == 0, POOL_WINDOWS[0],
                       jnp.where(group == 1, POOL_WINDOWS[1],
                                 jnp.where(group == 2, POOL_WINDOWS[2], POOL_WINDOWS[3])))
    pos = pos0 + t * tl + lax.broadcasted_iota(jnp.int32, (tl, width), 0)
    cnt = jnp.minimum(pos + 1, window).astype(jnp.float32)
    d = wsum / cnt - u
    o_ref[0] = _dot(d.astype(jnp.bfloat16), w_ref[...]) * scale_ref[...]


def _pool(u, hist0, w_bd, scale, pos0, tl):
    b, l, w = u.shape
    return pl.pallas_call(
        functools.partial(_pool_kernel, pos0=pos0, tl=tl),
        grid=(b, l // tl),
        in_specs=[pl.BlockSpec((1, tl, w), lambda i, t: (i, t, 0)),
                  pl.BlockSpec((1, POOL_HIST, w), lambda i, t: (i, 0, 0)),
                  _const_spec((w, w)), _const_spec((1, w))],
        out_specs=pl.BlockSpec((1, tl, w), lambda i, t: (i, t, 0)),
        out_shape=jax.ShapeDtypeStruct((b, l, w), jnp.float32),
        scratch_shapes=[pltpu.VMEM((POOL_HIST, w), jnp.float32)],
        compiler_params=_cparams(("arbitrary", "arbitrary")),
    )(u, hist0, w_bd, scale)


def _tail_kernel(x_ref, oa_ref, of_ref, oc_ref, wo_ref, gf_ref, wg_ref, wu_ref, wd_ref, gl_ref, o_ref,
                 *, ff_chunks, final_norm):
    bf = jnp.bfloat16
    mixed = jnp.concatenate([oa_ref[...], of_ref[...], oc_ref[...]], axis=1).astype(bf)
    x = x_ref[...] + _dot(mixed, wo_ref[...])
    h = (x * lax.rsqrt(jnp.mean(x * x, axis=-1, keepdims=True) + RMS_EPS) * gf_ref[...]).astype(bf)
    for c0, c1 in ff_chunks:
        act = _silu(_dot(h, wg_ref[:, c0:c1])) * _dot(h, wu_ref[:, c0:c1])
        x = x + _dot(act.astype(bf), wd_ref[c0:c1, :])
    if final_norm:
        x = x * lax.rsqrt(jnp.mean(x * x, axis=-1, keepdims=True) + RMS_EPS) * gl_ref[...]
    o_ref[...] = x


def _tail(x2d, oa, of, oc, wo, g_ffn, wg, wu, wd, g_last, final_norm, tm):
    n, d = x2d.shape
    d_ff = wg.shape[1]
    step = 1024
    ff_chunks = tuple((c0, min(c0 + step, d_ff)) for c0 in range(0, d_ff, step))
    row_spec = lambda w: pl.BlockSpec((tm, w), lambda i: (i, 0))
    weights = (wo, g_ffn, wg, wu, wd, g_last)
    return pl.pallas_call(
        functools.partial(_tail_kernel, ff_chunks=ff_chunks, final_norm=final_norm),
        grid=(n // tm,),
        in_specs=[row_spec(d), row_spec(oa.shape[1]), row_spec(of.shape[1]), row_spec(oc.shape[1])]
        + [_const_spec(a.shape) for a in weights],
        out_specs=row_spec(d),
        out_shape=jax.ShapeDtypeStruct((n, d), jnp.float32),
        compiler_params=_cparams(("arbitrary",)),
    )(x2d, oa, of, oc, *weights)


def _layer_weights(l, w_gdn, w_fox, w_pool, norm_mix_g, w_in_t, conv_w, a_log, dt_bias, gdn_norm_g,
                   fox_bf, pool_w, pool_scale, w_out, norm_ffn_g, w_gate_up, w_down):
    bf = jnp.bfloat16
    n_hg = w_gdn // HEAD_DIM
    n_hf = w_fox // HEAD_DIM
    d_ff = w_down.shape[1]
    o = 0
    src = {}
    for name, width in (("qkv", 3 * w_gdn), ("z", w_gdn), ("beta", n_hg), ("alpha", n_hg),
                        ("fq", w_fox), ("fk", w_fox), ("fv", w_fox), ("fgate", n_hf), ("pu", w_pool)):
        src[name] = (o, o + width)
        o += width
    wl = w_in_t[:, l, :]
    rows = lambda name: wl[src[name][0]:src[name][1]]
    gate_rows = jnp.concatenate([rows("beta"), rows("alpha"), rows("fgate")], axis=0)
    gate_rows = jnp.pad(gate_rows, ((0, LANES - gate_rows.shape[0]), (0, 0)))
    order = ("qkv", "z", "fq", "fk", "fv", "pu")
    w_t = jnp.concatenate([rows(n) for n in order] + [gate_rows], axis=0).astype(bf)
    seg, o = {}, 0
    for name in order:
        width = src[name][1] - src[name][0]
        seg[name] = (o, o + width)
        o += width
    seg["gates"] = (o, o + LANES)
    gp = jnp.zeros((8, LANES), jnp.float32)
    gp = gp.at[0, GATE_G:GATE_G + n_hg].set(a_log[l])
    gp = gp.at[1, GATE_G:GATE_G + n_hg].set(dt_bias[l])
    gp = gp.at[1, GATE_LOGF:GATE_LOGF + n_hf].set(fox_bf[l])
    n_groups, pg, _ = pool_w[l].shape
    w_bd = jnp.zeros((w_pool, w_pool), jnp.float32)
    for gi in range(n_groups):
        w_bd = w_bd.at[gi * pg:(gi + 1) * pg, gi * pg:(gi + 1) * pg].set(pool_w[l, gi])
    return dict(
        g_mix=norm_mix_g[l][None, :], w_t=w_t, seg=seg, gp=gp, conv_w=conv_w[l],
        gdn_g=gdn_norm_g[l][None, :], w_bd=w_bd.astype(bf), pool_scale=pool_scale[l][None, :],
        wo=w_out[l].astype(bf),
        g_ffn=norm_ffn_g[l][None, :], wg=w_gate_up[l][:, :d_ff].astype(bf),
        wu=w_gate_up[l][:, d_ff:].astype(bf), wd=w_down[l].astype(bf))


def kernel(x_prompt, x_sample, cache_k, cache_v, cache_logf, state_gdn, state_conv, state_pool,
           page_table, norm_mix_g, w_in, conv_w, a_log, dt_bias, gdn_norm_g, fox_bf, pool_w,
           pool_scale, w_out, norm_ffn_g, w_gate_up, w_down, final_norm_g):
    f32 = jnp.float32
    bp, seq, d = x_prompt.shape
    bs, dec_seq, _ = x_sample.shape
    depth, n_pool, page, n_hf, _ = cache_k.shape
    n_hg = state_gdn.shape[2]
    w_gdn = n_hg * HEAD_DIM
    w_fox = n_hf * HEAD_DIM
    w_pool = state_pool.shape[-1]
    pool_buf = state_pool.shape[2]
    conv_hist = state_conv.shape[2]
    past_len = page_table.shape[1] * page
    C = GDN_CHUNK
    assert dec_seq == 1 and seq % LANES == 0 and conv_hist == 3 and pool_buf == POOL_HIST - 1
    assert bs <= LANES

    tm_p = min(512, seq)
    tk = min(512, seq)
    tq = max(n * tk for n in (4, 2, 1) if seq % (n * tk) == 0)
    tl = min(512, seq)
    pps = max(p for p in (32, 16, 8, 4, 2, 1) if page_table.shape[1] % p == 0)
    g_last = final_norm_g[None, :]

    xp = x_prompt.reshape(bp * seq, d)
    xs = x_sample.reshape(bs, d)
    cache_kt = jnp.transpose(cache_k, (0, 1, 3, 4, 2)).reshape(depth, n_pool, w_fox, page)
    cache_vt = jnp.transpose(cache_v, (0, 1, 3, 4, 2)).reshape(depth, n_pool, w_fox, page)
    cache_lf_t = jnp.swapaxes(cache_logf, 2, 3)
    w_in_t = jnp.transpose(w_in, (2, 0, 1))

    st_p, st_s = [], []
    for l in range(depth):
        lw = _layer_weights(l, w_gdn, w_fox, w_pool, norm_mix_g, w_in_t, conv_w, a_log, dt_bias,
                            gdn_norm_g, fox_bf, pool_w, pool_scale, w_out, norm_ffn_g, w_gate_up, w_down)
        final = l == depth - 1

        qkv, z, fq, fkt, fvt, pu, gt, cum, lft, cumt = _in_proj(
            xp.reshape(bp, seq, d), lw["g_mix"], lw["w_t"], lw["gp"], lw["seg"], tm_p)
        o_a, s_fin = _gdn(qkv, z, gt, lw["conv_w"], jnp.zeros((bp, 8, 3 * w_gdn), f32),
                          jnp.zeros((bp, n_hg, HEAD_DIM, HEAD_DIM), f32), lw["gdn_g"], C)
        o_f = _fox_prompt(fq, fkt, fvt, cum, cumt, tq, tk)
        o_c = _pool(pu, jnp.zeros((bp, POOL_HIST, w_pool), f32), lw["w_bd"], lw["pool_scale"], 0, tl)
        xp = _tail(xp, o_a.reshape(bp * seq, w_gdn), o_f.reshape(bp * seq, w_fox),
                   o_c.reshape(bp * seq, w_pool), lw["wo"], lw["g_ffn"],
                   lw["wg"], lw["wu"], lw["wd"], g_last, final, tm_p)
        heads_last = lambda a: jnp.transpose(a.reshape(bp, n_hf, HEAD_DIM, seq), (0, 3, 1, 2))
        st_p.append((heads_last(fkt), heads_last(fvt), jnp.swapaxes(lft[:, :n_hf, :], 1, 2), s_fin,
                     qkv[:, seq - conv_hist:], pu[:, seq - pool_buf:]))

        xs_rows = jnp.pad(xs, ((0, LANES - bs), (0, 0)))[None]
        qkv, z, fq, fkt, fvt, pu, gt, _, _, _ = _in_proj(xs_rows, lw["g_mix"], lw["w_t"], lw["gp"],
                                                         lw["seg"], LANES)
        qkv, z, fq, pu, gt = (a[0, :bs] for a in (qkv, z, fq, pu, gt))
        fk, fv = (jnp.swapaxes(a[0, :, :bs], 0, 1) for a in (fkt, fvt))
        pad_c = lambda a: jnp.pad(a[:, None, :], ((0, 0), (0, C - 1), (0, 0)))
        conv0 = jnp.pad(state_conv[l], ((0, 0), (8 - conv_hist, 0), (0, 0)))
        o_a, s_fin = _gdn(pad_c(qkv), pad_c(z), pad_c(gt), lw["conv_w"], conv0, state_gdn[l],
                          lw["gdn_g"], 1)
        o_f = _fox_sample(page_table, fq[:, None, :], fk[:, None, :], fv[:, None, :], gt[:, None, :],
                          cache_kt, cache_vt, cache_lf_t, l, pps)
        hist0 = jnp.pad(state_pool[l], ((0, 0), (POOL_HIST - pool_buf, 0), (0, 0)))
        o_c = _pool(jnp.pad(pu[:, None, :], ((0, 0), (0, 7), (0, 0))), hist0, lw["w_bd"],
                    lw["pool_scale"], past_len, 8)
        xs = _tail(xs, o_a[:, 0], o_f[:, 0], o_c[:, 0], lw["wo"], lw["g_ffn"],
                   lw["wg"], lw["wu"], lw["wd"], g_last, final, bs)
        st_s.append((fk.reshape(bs, 1, n_hf, HEAD_DIM), fv.reshape(bs, 1, n_hf, HEAD_DIM),
                     gt[:, None, GATE_LOGF:GATE_LOGF + n_hf], s_fin,
                     jnp.concatenate([state_conv[l][:, 1:], qkv[:, None, :]], axis=1),
                     jnp.concatenate([state_pool[l][:, 1:], pu[:, None, :]], axis=1)))

    outs = [xp.reshape(bp, seq, d), xs.reshape(bs, 1, d)]
    for st in (st_p, st_s):
        for i in range(6):
            outs.append(jnp.stack([s[i] for s in st]))
    return tuple(outs)
```
